```python
import math
import jax, jax.numpy as jnp
from jax import lax
import numpy as np

D_MODEL = 2048
BATCH = 4
SEQ = 4096
DEPTH = 2

HEAD_DIM = 128
N_HEADS = D_MODEL // HEAD_DIM
N_HEADS_FOX = N_HEADS // 2
N_HEADS_DIL = N_HEADS - N_HEADS_FOX
D_FOX = N_HEADS_FOX * HEAD_DIM
D_DIL = N_HEADS_DIL * HEAD_DIM
D_MIX = D_FOX + D_DIL
SPLIT_SIZES = (D_FOX, D_FOX, D_FOX, N_HEADS_FOX, D_DIL, D_DIL, D_DIL)
SPLIT_POINTS = tuple(int(s) for s in np.cumsum(SPLIT_SIZES)[:-1])
N_IN = int(sum(SPLIT_SIZES))
D_FF = 4 * D_MODEL
Q_BLOCK = 128
DIL_PATTERNS = ((128, 1), (512, 4), (2048, 16))
DIL_BLOCK = 128
REL_BUCKETS = 32
REL_MAX_DISTANCE = 2048
NORM_EPS = 1e-6
NEG_INF = -1e30

kernel_name = "hymba_fox_dilated_hybrid"


def rms_norm(x, g):
    xf = x.astype(jnp.float32)
    y = xf * lax.rsqrt(jnp.mean(xf * xf, axis=-1, keepdims=True) + NORM_EPS)
    return (y * g.astype(jnp.float32)).astype(x.dtype)


def fox_attention(q, k, v, log_f):
    B, S, H, E = q.shape
    scale = E ** -0.5
    c = jnp.cumsum(log_f, axis=1).transpose(0, 2, 1)
    qh = q.transpose(0, 2, 1, 3)
    kh = k.transpose(0, 2, 1, 3)
    vh = v.transpose(0, 2, 1, 3)
    k_pos = jnp.arange(S)
    n_blk = S // Q_BLOCK

    def block(i):
        start = i * Q_BLOCK
        qb = lax.dynamic_slice_in_dim(qh, start, Q_BLOCK, axis=2)
        cb = lax.dynamic_slice_in_dim(c, start, Q_BLOCK, axis=2)
        logits = jnp.einsum('bhqe,bhke->bhqk', qb, kh,
                            preferred_element_type=jnp.float32) * scale
        logits = logits + (cb[..., :, None] - c[..., None, :])
        q_pos = start + jnp.arange(Q_BLOCK)
        causal = k_pos[None, :] <= q_pos[:, None]
        logits = jnp.where(causal, logits, NEG_INF)
        p = jax.nn.softmax(logits, axis=-1)
        return jnp.einsum('bhqk,bhke->bqhe', p.astype(v.dtype), vh)

    out = lax.map(block, jnp.arange(n_blk))
    return out.transpose(1, 0, 2, 3, 4).reshape(B, S, H, E)


def rel_bucket(dist):
    max_exact = REL_BUCKETS // 2
    d = jnp.maximum(dist.astype(jnp.float32), 1.0)
    large = max_exact + (jnp.log(d / max_exact) / math.log(REL_MAX_DISTANCE / max_exact)
                         * (REL_BUCKETS - max_exact)).astype(jnp.int32)
    large = jnp.minimum(large, REL_BUCKETS - 1)
    return jnp.where(dist < max_exact, dist, large)


def dilated_pattern(q, k, v, rel_bias, window, dilation):
    B, S, H, E = q.shape
    scale = E ** -0.5
    span = window // dilation
    group = dilation * DIL_BLOCK
    s_pad = -(-S // group) * group
    pad = s_pad - S
    n_sub = s_pad // dilation
    n_blk = n_sub // DIL_BLOCK

    def to_blocks(t):
        t = jnp.pad(t, ((0, 0), (0, pad), (0, 0), (0, 0)))
        t = t.reshape(B, n_sub, dilation, H, E).transpose(0, 2, 3, 1, 4)
        return t.reshape(B, dilation, H, n_blk, DIL_BLOCK, E)

    def with_prev(t):
        prev = jnp.pad(t, ((0, 0), (0, 0), (0, 0), (1, 0), (0, 0), (0, 0)))[:, :, :, :-1]
        return jnp.concatenate([prev, t], axis=4)

    qb = to_blocks(q)
    kc = with_prev(to_blocks(k))
    vc = with_prev(to_blocks(v))
    logits = jnp.einsum('bdhnqe,bdhnke->bdhnqk', qb, kc,
                        preferred_element_type=jnp.float32) * scale
    i = jnp.arange(DIL_BLOCK)[:, None]
    j = jnp.arange(2 * DIL_BLOCK)[None, :]
    rel = DIL_BLOCK + i - j
    in_band = (rel >= 0) & (rel <= span)
    bias = rel_bias.astype(jnp.float32)[rel_bucket(jnp.clip(rel, 0, span) * dilation)]
    bias = bias.transpose(2, 0, 1)
    key_valid = (jnp.arange(n_blk)[:, None, None] > 0) | (j[None] >= DIL_BLOCK)
    mask = in_band[None] & key_valid
    logits = logits + bias[None, None, :, None]
    logits = jnp.where(mask[None, None, None], logits, NEG_INF)
    m = jnp.max(logits, axis=-1, keepdims=True)
    p = jnp.exp(logits - m)
    s = jnp.sum(p, axis=-1, keepdims=True)
    o = jnp.einsum('bdhnqk,bdhnke->bdhnqe', p, vc.astype(jnp.float32)) / s
    lse = (m + jnp.log(s))[..., 0]

    def from_blocks(t):
        tail = t.shape[5:]
        t = t.reshape((B, dilation, H, n_sub) + tail)
        t = jnp.moveaxis(t, 3, 1)
        return t.reshape((B, s_pad, H) + tail)[:, :S]

    return from_blocks(o), from_blocks(lse)


def dilated_attention(q, k, v, rel_bias):
    outs, lses = [], []
    for window, dilation in DIL_PATTERNS:
        o, l = dilated_pattern(q, k, v, rel_bias, window, dilation)
        outs.append(o)
        lses.append(l)
    alpha = jax.nn.softmax(jnp.stack(lses, axis=0), axis=0)
    return jnp.einsum('pbsh,pbshe->bshe', alpha, jnp.stack(outs, axis=0))


def setup_inputs(seed: int = 0) -> dict:
    key = jax.random.key(seed)
    ks = jax.random.split(key, 12)
    f32 = jnp.float32
    return {
        "x": jax.random.normal(ks[0], (BATCH, SEQ, D_MODEL), f32),
        "norm1_g": 1.0 + 0.02 * jax.random.normal(ks[1], (DEPTH, D_MODEL), f32),
        "w_in": jax.random.normal(ks[2], (DEPTH, D_MODEL, N_IN), f32) * D_MODEL ** -0.5,
        "forget_b": jax.random.uniform(ks[3], (DEPTH, N_HEADS_FOX), f32, 1.0, 4.0),
        "rel_bias": 0.5 * jax.random.normal(ks[4], (REL_BUCKETS, N_HEADS_DIL), f32),
        "outnorm_a_g": 1.0 + 0.02 * jax.random.normal(ks[5], (DEPTH, D_FOX), f32),
        "outnorm_b_g": 1.0 + 0.02 * jax.random.normal(ks[6], (DEPTH, D_DIL), f32),
        "w_out": jax.random.normal(ks[7], (DEPTH, D_MIX, D_MODEL), f32) * D_MIX ** -0.5,
        "norm2_g": 1.0 + 0.02 * jax.random.normal(ks[8], (DEPTH, D_MODEL), f32),
        "w_mlp_in": jax.random.normal(ks[9], (DEPTH, D_MODEL, D_FF), f32) * D_MODEL ** -0.5,
        "w_mlp_out": jax.random.normal(ks[10], (DEPTH, D_FF, D_MODEL), f32) * D_FF ** -0.5,
        "final_norm_g": 1.0 + 0.02 * jax.random.normal(ks[11], (D_MODEL,), f32),
    }


def reference(x, norm1_g, w_in, forget_b, rel_bias, outnorm_a_g, outnorm_b_g, w_out,
              norm2_g, w_mlp_in, w_mlp_out, final_norm_g):
    B, S, _ = x.shape
    for l in range(DEPTH):
        h = rms_norm(x, norm1_g[l])
        proj = h @ w_in[l]
        q_a, k_a, v_a, f_a, q_b, k_b, v_b = jnp.split(proj, SPLIT_POINTS, axis=-1)
        heads_a = lambda t: t.reshape(B, S, N_HEADS_FOX, HEAD_DIM)
        heads_b = lambda t: t.reshape(B, S, N_HEADS_DIL, HEAD_DIM)
        log_f = jax.nn.log_sigmoid(f_a.astype(jnp.float32) + forget_b[l].astype(jnp.float32))
        y_a = fox_attention(heads_a(q_a), heads_a(k_a), heads_a(v_a), log_f)
        y_a = y_a.reshape(B, S, D_FOX)
        y_b = dilated_attention(heads_b(q_b), heads_b(k_b), heads_b(v_b), rel_bias)
        y_b = y_b.reshape(B, S, D_DIL).astype(x.dtype)
        mixed = jnp.concatenate([rms_norm(y_a, outnorm_a_g[l]),
                                 rms_norm(y_b, outnorm_b_g[l])], axis=-1)
        x = x + mixed @ w_out[l]
        h = rms_norm(x, norm2_g[l])
        x = x + jnp.square(jax.nn.relu(h @ w_mlp_in[l])) @ w_mlp_out[l]
    return rms_norm(x, final_norm_g)
```

```python
import functools
import math

import numpy as np
import jax
import jax.numpy as jnp
from jax import lax
from jax.experimental import pallas as pl
from jax.experimental.pallas import tpu as pltpu

F32 = jnp.float32
BF16 = jnp.bfloat16

HEAD_DIM = 128
N_HEADS_FOX = 8
N_HEADS_DIL = 8
D_FOX = N_HEADS_FOX * HEAD_DIM
D_DIL = N_HEADS_DIL * HEAD_DIM
DIL_PATTERNS = ((128, 1), (512, 4), (2048, 16))
DIL_BLOCK = 128
MAX_DIL = 16
REL_BUCKETS = 32
REL_MAX_DISTANCE = 2048
NORM_EPS = 1e-6
NEG_INF = -1e30

LANES = 128
VMEM_LIMIT = 56 * 1024 * 1024


def _cparams(sem):
    return pltpu.CompilerParams(dimension_semantics=sem, vmem_limit_bytes=VMEM_LIMIT)


def _rms_scale(x):
    return lax.rsqrt(jnp.mean(x * x, axis=-1, keepdims=True) + NORM_EPS)


def _in_proj_kernel(x_ref, g_ref, w_ref, wf_ref, fb_ref, fox_ref, dil_ref, logf_ref, xn_ref,
                    *, n_fox_tiles):
    j = pl.program_id(1)

    @pl.when(j == 0)
    def _():
        x = x_ref[...]
        xn = (x * _rms_scale(x) * g_ref[...]).astype(BF16)
        xn_ref[...] = xn
        f = jnp.dot(xn, wf_ref[...], preferred_element_type=F32) + fb_ref[...]
        logf_ref[...] = jnp.minimum(f, 0.0) - jnp.log1p(jnp.exp(-jnp.abs(f)))

    acc = jnp.dot(xn_ref[...], w_ref[...], preferred_element_type=F32)

    @pl.when(j < n_fox_tiles)
    def _():
        fox_ref[...] = acc.astype(BF16)

    @pl.when(j >= n_fox_tiles)
    def _():
        dil_ref[...] = acc


def _in_proj(x, g, w_qkv, w_f, fb, *, tm=512, tn=512):
    T, D = x.shape
    n_fox = 3 * D_FOX
    n_dil = 3 * D_DIL
    nf, nd = n_fox // tn, n_dil // tn
    kern = functools.partial(_in_proj_kernel, n_fox_tiles=nf)
    return pl.pallas_call(
        kern,
        grid=(T // tm, nf + nd),
        in_specs=[
            pl.BlockSpec((tm, D), lambda i, j: (i, 0)),
            pl.BlockSpec((1, D), lambda i, j: (0, 0)),
            pl.BlockSpec((D, tn), lambda i, j: (0, j)),
            pl.BlockSpec((D, LANES), lambda i, j: (0, 0)),
            pl.BlockSpec((1, LANES), lambda i, j: (0, 0)),
        ],
        out_specs=[
            pl.BlockSpec((tm, tn), lambda i, j: (i, jnp.minimum(j, nf - 1))),
            pl.BlockSpec((tm, tn), lambda i, j: (i, jnp.maximum(j - nf, 0))),
            pl.BlockSpec((tm, LANES), lambda i, j: (i, 0)),
        ],
        out_shape=[
            jax.ShapeDtypeStruct((T, n_fox), BF16),
            jax.ShapeDtypeStruct((T, n_dil), F32),
            jax.ShapeDtypeStruct((T, LANES), F32),
        ],
        scratch_shapes=[pltpu.VMEM((tm, D), BF16)],
        compiler_params=_cparams(("parallel", "arbitrary")),
        name="in_proj",
    )(x, g, w_qkv, w_f, fb)


def _split3(x):
    hi = x.astype(BF16)
    r1 = x - hi.astype(F32)
    mid = r1.astype(BF16)
    lo = (r1 - mid.astype(F32)).astype(BF16)
    return hi, mid, lo


def _cumsum_kernel(logf_ref, c_ref, cs_ref, *, n_heads, chunk):
    S = logf_ref.shape[0]
    n_chunks = S // chunk
    row = lax.broadcasted_iota(jnp.int32, (chunk, chunk), 0)
    col = lax.broadcasted_iota(jnp.int32, (chunk, chunk), 1)
    tri = (col <= row).astype(BF16)
    carry = jnp.zeros((1, LANES), F32)
    for ci in range(n_chunks):
        seg = logf_ref[ci * chunk:(ci + 1) * chunk, :]
        hi, mid, lo = _split3(seg)
        within = (jnp.dot(tri, hi, preferred_element_type=F32)
                  + jnp.dot(tri, mid, preferred_element_type=F32)
                  + jnp.dot(tri, lo, preferred_element_type=F32))
        within = within + carry
        cs_ref[ci * chunk:(ci + 1) * chunk, :] = within
        carry = within[chunk - 1:chunk, :]
    ct = cs_ref[...].T
    for h in range(n_heads):
        c_ref[h, :, :] = ct[h:h + 1, :]


def _fox_cumsum(logf, B, S):
    kern = functools.partial(_cumsum_kernel, n_heads=N_HEADS_FOX, chunk=128)
    return pl.pallas_call(
        kern,
        grid=(B,),
        in_specs=[pl.BlockSpec((S, LANES), lambda b: (b, 0))],
        out_specs=pl.BlockSpec((None, N_HEADS_FOX, 1, S), lambda b: (b, 0, 0, 0)),
        out_shape=jax.ShapeDtypeStruct((B, N_HEADS_FOX, 1, S), F32),
        scratch_shapes=[pltpu.VMEM((S, LANES), F32)],
        compiler_params=_cparams(("parallel",)),
        name="fox_cumsum",
    )(logf)


def _fox_kernel(q_ref, k_ref, v_ref, c_ref, o_ref, m_ref, l_ref, acc_ref, *, tq, scale):
    qi = pl.program_id(2)
    q = q_ref[...]
    m_ref[...] = jnp.full(m_ref.shape, NEG_INF, F32)
    l_ref[...] = jnp.zeros(l_ref.shape, F32)
    acc_ref[...] = jnp.zeros(acc_ref.shape, F32)

    def step(kb, masked):
        start = pl.multiple_of(kb * tq, tq)
        k = k_ref[pl.ds(start, tq), :]
        v = v_ref[pl.ds(start, tq), :]
        ck = c_ref[:, pl.ds(start, tq)]
        s = lax.dot_general(q, k, (((1,), (1,)), ((), ())), preferred_element_type=F32)
        s = s * scale - ck
        if masked:
            row = lax.broadcasted_iota(jnp.int32, (tq, tq), 0)
            col = lax.broadcasted_iota(jnp.int32, (tq, tq), 1)
            s = jnp.where(col <= row, s, NEG_INF)
        m_prev = m_ref[...]
        m_new = jnp.maximum(m_prev, jnp.max(s, axis=-1, keepdims=True))
        alpha = jnp.exp(m_prev - m_new)
        p = jnp.exp(s - m_new)
        l_ref[...] = alpha * l_ref[...] + jnp.sum(p, axis=-1, keepdims=True)
        acc_ref[...] = alpha * acc_ref[...] + jnp.dot(p.astype(BF16), v, preferred_element_type=F32)
        m_ref[...] = m_new

    def body(kb, carry):
        step(kb, masked=False)
        return carry

    lax.fori_loop(0, qi, body, 0)
    step(qi, masked=True)
    o_ref[...] = acc_ref[...] / l_ref[...]


def _fox_attention(fox, c, B, S, *, tq=512):
    H = N_HEADS_FOX
    kern = functools.partial(_fox_kernel, tq=tq, scale=HEAD_DIM ** -0.5)
    return pl.pallas_call(
        kern,
        grid=(B, H, S // tq),
        in_specs=[
            pl.BlockSpec((None, tq, HEAD_DIM), lambda b, h, i: (b, i, h)),
            pl.BlockSpec((None, S, HEAD_DIM), lambda b, h, i: (b, 0, H + h)),
            pl.BlockSpec((None, S, HEAD_DIM), lambda b, h, i: (b, 0, 2 * H + h)),
            pl.BlockSpec((None, None, 1, S), lambda b, h, i: (b, h, 0, 0)),
        ],
        out_specs=pl.BlockSpec((None, tq, HEAD_DIM), lambda b, h, i: (b, i, h)),
        out_shape=jax.ShapeDtypeStruct((B, S, D_FOX), F32),
        scratch_shapes=[
            pltpu.VMEM((tq, 1), F32),
            pltpu.VMEM((tq, 1), F32),
            pltpu.VMEM((tq, HEAD_DIM), F32),
        ],
        compiler_params=_cparams(("parallel", "parallel", "arbitrary")),
        name="fox_attention",
    )(fox, fox, fox, c)


def _rel_bucket_np(dist):
    max_exact = REL_BUCKETS // 2
    d = np.maximum(dist.astype(np.float32), np.float32(1.0))
    ratio = np.log(d / np.float32(max_exact)).astype(np.float32) / np.float32(
        math.log(REL_MAX_DISTANCE / max_exact))
    large = max_exact + (ratio * np.float32(REL_BUCKETS - max_exact)).astype(np.int32)
    large = np.minimum(large, REL_BUCKETS - 1)
    return np.where(dist < max_exact, dist, large)


def _block_perm(dilation):
    n_chunks = MAX_DIL // dilation
    rows = DIL_BLOCK // n_chunks
    a = np.arange(n_chunks)[:, None]
    nn = np.arange(rows)[None, :]
    return (nn * n_chunks + a).reshape(-1)


def _bucket_tables():
    tabs = []
    for window, dilation in DIL_PATTERNS:
        span = window // dilation
        perm = _block_perm(dilation)
        i = perm[:, None]
        j = np.concatenate([perm, DIL_BLOCK + perm])[None, :]
        rel = DIL_BLOCK + i - j
        in_band = (rel >= 0) & (rel <= span)
        bucket = _rel_bucket_np(np.clip(rel, 0, span) * dilation)
        tabs.append(np.where(in_band, bucket, -1).astype(np.int32))
    return np.stack(tabs)


def _bias_table_kernel(rel_bias_ref, idx_ref, tab_ref):
    h = pl.program_id(1)
    idx = idx_ref[...]
    tab = jnp.full(idx.shape, NEG_INF, F32)
    for b in range(REL_BUCKETS):
        tab = jnp.where(idx == b, rel_bias_ref[b, h], tab)
    tab_ref[...] = tab


def _bias_tables(rel_bias):
    idx = jnp.asarray(_bucket_tables())
    P = len(DIL_PATTERNS)
    return pl.pallas_call(
        _bias_table_kernel,
        grid=(P, N_HEADS_DIL),
        in_specs=[
            pl.BlockSpec(memory_space=pltpu.SMEM),
            pl.BlockSpec((None, DIL_BLOCK, 2 * DIL_BLOCK), lambda p, h: (p, 0, 0)),
        ],
        out_specs=pl.BlockSpec((None, None, DIL_BLOCK, 2 * DIL_BLOCK), lambda p, h: (p, h, 0, 0)),
        out_shape=jax.ShapeDtypeStruct((P, N_HEADS_DIL, DIL_BLOCK, 2 * DIL_BLOCK), F32),
        compiler_params=_cparams(("parallel", "parallel")),
        name="dil_bias_tables",
    )(rel_bias, idx)


def _dil_kernel(q_ref, k_ref, v_ref, tab_ref, o_ref, m_ref, l_ref, acc_ref, *, scale):
    n_sub = q_ref.shape[1]

    def gather(ref, dilation, r0, blk, n_blocks=1):
        n_chunks = MAX_DIL // dilation
        rows = DIL_BLOCK // n_chunks
        parts = []
        for b in range(n_blocks):
            start = pl.multiple_of((blk + b) * rows, rows)
            for a in range(n_chunks):
                parts.append(ref[r0 + a * dilation, pl.ds(start, rows), :])
        return parts[0] if len(parts) == 1 else jnp.concatenate(parts, axis=0)

    def scatter(ref, val, dilation, r0, blk):
        n_chunks = MAX_DIL // dilation
        rows = DIL_BLOCK // n_chunks
        start = pl.multiple_of(blk * rows, rows)
        for a in range(n_chunks):
            ref[r0 + a * dilation, pl.ds(start, rows), :] = val[a * rows:(a + 1) * rows]

    def block(p_idx, dilation, r0, blk, first_block, first_pattern, last_pattern):
        q = gather(q_ref, dilation, r0, blk).astype(BF16)
        if first_block:
            k = gather(k_ref, dilation, r0, blk).astype(BF16)
            v = gather(v_ref, dilation, r0, blk).astype(BF16)
            tab = tab_ref[p_idx, :, DIL_BLOCK:]
        else:
            k = gather(k_ref, dilation, r0, blk - 1, 2).astype(BF16)
            v = gather(v_ref, dilation, r0, blk - 1, 2).astype(BF16)
            tab = tab_ref[p_idx]
        s = lax.dot_general(q, k, (((1,), (1,)), ((), ())), preferred_element_type=F32)
        s = s * scale + tab
        m1 = jnp.max(s, axis=-1, keepdims=True)
        p = jnp.exp(s - m1)
        l1 = jnp.sum(p, axis=-1, keepdims=True)
        o1 = jnp.dot(p.astype(BF16), v, preferred_element_type=F32)
        if first_pattern:
            m, l, acc = m1, l1, o1
        else:
            m0 = gather(m_ref, dilation, r0, blk)
            l0 = gather(l_ref, dilation, r0, blk)
            acc0 = gather(acc_ref, dilation, r0, blk)
            m = jnp.maximum(m0, m1)
            a0 = jnp.exp(m0 - m)
            a1 = jnp.exp(m1 - m)
            l = a0 * l0 + a1 * l1
            acc = a0 * acc0 + a1 * o1
        if last_pattern:
            scatter(o_ref, acc / l, dilation, r0, blk)
        else:
            scatter(m_ref, m, dilation, r0, blk)
            scatter(l_ref, l, dilation, r0, blk)
            scatter(acc_ref, acc, dilation, r0, blk)

    order = sorted(range(len(DIL_PATTERNS)), key=lambda i: -DIL_PATTERNS[i][1])
    for pos, p_idx in enumerate(order):
        dilation = DIL_PATTERNS[p_idx][1]
        n_blocks = (n_sub * MAX_DIL // dilation) // DIL_BLOCK
        first_pattern = pos == 0
        last_pattern = pos == len(order) - 1
        for r0 in range(dilation):
            block(p_idx, dilation, r0, 0, True, first_pattern, last_pattern)

            def body(blk, carry, r0=r0, p_idx=p_idx, dilation=dilation,
                     first_pattern=first_pattern, last_pattern=last_pattern):
                block(p_idx, dilation, r0, blk, False, first_pattern, last_pattern)
                return carry

            lax.fori_loop(1, n_blocks, body, 0)


def _dil_attention(qkv, tabs, B, S):
    H = N_HEADS_DIL
    n_sub = S // MAX_DIL
    P = len(DIL_PATTERNS)
    kern = functools.partial(_dil_kernel, scale=HEAD_DIM ** -0.5)
    blk = (None, MAX_DIL, n_sub, HEAD_DIM)
    return pl.pallas_call(
        kern,
        grid=(B, H),
        in_specs=[
            pl.BlockSpec(blk, lambda b, h: (b, 0, 0, h)),
            pl.BlockSpec(blk, lambda b, h: (b, 0, 0, H + h)),
            pl.BlockSpec(blk, lambda b, h: (b, 0, 0, 2 * H + h)),
            pl.BlockSpec((P, None, DIL_BLOCK, 2 * DIL_BLOCK), lambda b, h: (0, h, 0, 0)),
        ],
        out_specs=pl.BlockSpec(blk, lambda b, h: (b, 0, 0, h)),
        out_shape=jax.ShapeDtypeStruct((B, MAX_DIL, n_sub, D_DIL), F32),
        scratch_shapes=[
            pltpu.VMEM((MAX_DIL, n_sub, 1), F32),
            pltpu.VMEM((MAX_DIL, n_sub, 1), F32),
            pltpu.VMEM((MAX_DIL, n_sub, HEAD_DIM), F32),
        ],
        compiler_params=_cparams(("parallel", "parallel")),
        name="dil_attention",
    )(qkv, qkv, qkv, tabs)


def _out_proj_kernel(ya_ref, yb_ref, ga_ref, gb_ref, w_ref, x_ref, o_ref, mix_ref):
    j = pl.program_id(1)

    @pl.when(j == 0)
    def _():
        ya = ya_ref[...]
        yb = yb_ref[...]
        mix_ref[:, :D_FOX] = (ya * _rms_scale(ya) * ga_ref[...]).astype(BF16)
        mix_ref[:, D_FOX:] = (yb * _rms_scale(yb) * gb_ref[...]).astype(BF16)

    o_ref[...] = x_ref[...] + jnp.dot(mix_ref[...], w_ref[...], preferred_element_type=F32)


def _out_proj(ya, yb, ga, gb, w, x, *, tm=512, tn=512):
    T, D = x.shape
    return pl.pallas_call(
        _out_proj_kernel,
        grid=(T // tm, D // tn),
        in_specs=[
            pl.BlockSpec((tm, D_FOX), lambda i, j: (i, 0)),
            pl.BlockSpec((tm, D_DIL), lambda i, j: (i, 0)),
            pl.BlockSpec((1, D_FOX), lambda i, j: (0, 0)),
            pl.BlockSpec((1, D_DIL), lambda i, j: (0, 0)),
            pl.BlockSpec((D_FOX + D_DIL, tn), lambda i, j: (0, j)),
            pl.BlockSpec((tm, tn), lambda i, j: (i, j)),
        ],
        out_specs=pl.BlockSpec((tm, tn), lambda i, j: (i, j)),
        out_shape=jax.ShapeDtypeStruct((T, D), F32),
        scratch_shapes=[pltpu.VMEM((tm, D_FOX + D_DIL), BF16)],
        compiler_params=_cparams(("parallel", "arbitrary")),
        name="out_proj",
    )(ya, yb, ga, gb, w, x)


def _mlp_kernel(x_ref, g_ref, w1_ref, w2_ref, fg_ref, o_ref, xn_ref, acc_ref, *, final_norm):
    f = pl.program_id(1)

    @pl.when(f == 0)
    def _():
        x = x_ref[...]
        xn_ref[...] = (x * _rms_scale(x) * g_ref[...]).astype(BF16)
        acc_ref[...] = x

    h = jnp.dot(xn_ref[...], w1_ref[...], preferred_element_type=F32)
    h = jnp.square(jnp.maximum(h, 0.0))
    acc_ref[...] += jnp.dot(h.astype(BF16), w2_ref[...], preferred_element_type=F32)

    @pl.when(f == pl.num_programs(1) - 1)
    def _():
        y = acc_ref[...]
        if final_norm:
            y = y * _rms_scale(y) * fg_ref[...]
        o_ref[...] = y


def _mlp(x, g, w1, w2, fg, *, final_norm, tm=512, tf=512):
    T, D = x.shape
    F = w1.shape[1]
    kern = functools.partial(_mlp_kernel, final_norm=final_norm)
    return pl.pallas_call(
        kern,
        grid=(T // tm, F // tf),
        in_specs=[
            pl.BlockSpec((tm, D), lambda i, f: (i, 0)),
            pl.BlockSpec((1, D), lambda i, f: (0, 0)),
            pl.BlockSpec((D, tf), lambda i, f: (0, f)),
            pl.BlockSpec((tf, D), lambda i, f: (f, 0)),
            pl.BlockSpec((1, D), lambda i, f: (0, 0)),
        ],
        out_specs=pl.BlockSpec((tm, D), lambda i, f: (i, 0)),
        out_shape=jax.ShapeDtypeStruct((T, D), F32),
        scratch_shapes=[pltpu.VMEM((tm, D), BF16), pltpu.VMEM((tm, D), F32)],
        compiler_params=_cparams(("parallel", "arbitrary")),
        name="mlp",
    )(x, g, w1, w2, fg)


def kernel(x, norm1_g, w_in, forget_b, rel_bias, outnorm_a_g, outnorm_b_g, w_out,
           norm2_g, w_mlp_in, w_mlp_out, final_norm_g):
    B, S, D = x.shape
    T = B * S
    depth = w_in.shape[0]
    n_sub = S // MAX_DIL
    assert S % (MAX_DIL * DIL_BLOCK) == 0 and w_in.shape[2] == 3 * D_FOX + N_HEADS_FOX + 3 * D_DIL

    tabs = _bias_tables(rel_bias.astype(F32))
    fg = final_norm_g.reshape(1, D).astype(F32)
    xf = x.reshape(T, D)
    for l in range(depth):
        f0 = 3 * D_FOX
        w_qkv = jnp.concatenate([w_in[l, :, :f0], w_in[l, :, f0 + N_HEADS_FOX:]], axis=1).astype(BF16)
        w_f = jnp.pad(w_in[l, :, f0:f0 + N_HEADS_FOX], ((0, 0), (0, LANES - N_HEADS_FOX))).astype(BF16)
        fb = jnp.pad(forget_b[l].astype(F32), (0, LANES - N_HEADS_FOX)).reshape(1, LANES)

        fox, dil, logf = _in_proj(xf, norm1_g[l].reshape(1, D), w_qkv, w_f, fb)
        c = _fox_cumsum(logf, B, S)
        ya = _fox_attention(fox.reshape(B, S, 3 * D_FOX), c, B, S)

        dil = dil.reshape(B, n_sub, MAX_DIL, 3 * D_DIL).transpose(0, 2, 1, 3)
        yb = _dil_attention(dil, tabs, B, S)
        yb = yb.transpose(0, 2, 1, 3).reshape(T, D_DIL)

        xf = _out_proj(ya.reshape(T, D_FOX), yb, outnorm_a_g[l].reshape(1, D_FOX),
                       outnorm_b_g[l].reshape(1, D_DIL), w_out[l].astype(BF16), xf)
        xf = _mlp(xf, norm2_g[l].reshape(1, D), w_mlp_in[l].astype(BF16), w_mlp_out[l].astype(BF16),
                  fg, final_norm=(l == depth - 1))
    return xf.reshape(B, S, D)
```

```python
import functools
import math

import numpy as np
import jax
import jax.numpy as jnp
from jax import lax
from jax.experimental import pallas as pl
from jax.experimental.pallas import tpu as pltpu

F32 = jnp.float32
BF16 = jnp.bfloat16

HEAD_DIM = 128
N_HEADS_FOX = 8
N_HEADS_DIL = 8
D_FOX = N_HEADS_FOX * HEAD_DIM
D_DIL = N_HEADS_DIL * HEAD_DIM
DIL_PATTERNS = ((128, 1), (512, 4), (2048, 16))
DIL_BLOCK = 128
MAX_DIL = 16
REL_BUCKETS = 32
REL_MAX_DISTANCE = 2048
NORM_EPS = 1e-6
NEG_INF = -1e30
LOG2E = math.log2(math.e)

LANES = 128
VMEM_LIMIT = 56 * 1024 * 1024


def _cparams(sem):
    return pltpu.CompilerParams(dimension_semantics=sem, vmem_limit_bytes=VMEM_LIMIT)


def _rms_scale(x):
    return lax.rsqrt(jnp.mean(x * x, axis=-1, keepdims=True) + NORM_EPS)


def _in_proj_kernel(x_ref, g_ref, w_ref, wf_ref, fb_ref, fox_ref, dil_ref, logf_ref, xn_ref,
                    *, n_fox_tiles):
    j = pl.program_id(1)

    @pl.when(j == 0)
    def _():
        x = x_ref[...]
        xn = (x * _rms_scale(x) * g_ref[...]).astype(BF16)
        xn_ref[...] = xn
        f = jnp.dot(xn, wf_ref[...], preferred_element_type=F32) + fb_ref[...]
        logf_ref[...] = jnp.minimum(f, 0.0) - jnp.log1p(jnp.exp(-jnp.abs(f)))

    acc = jnp.dot(xn_ref[...], w_ref[...], preferred_element_type=F32)

    @pl.when(j < n_fox_tiles)
    def _():
        fox_ref[...] = acc.astype(BF16)

    @pl.when(j >= n_fox_tiles)
    def _():
        dil_ref[...] = acc


def _in_proj(x, g, w_qkv, w_f, fb, *, tm=512, tn=512):
    T, D = x.shape
    n_fox = 3 * D_FOX
    n_dil = 3 * D_DIL
    nf, nd = n_fox // tn, n_dil // tn
    kern = functools.partial(_in_proj_kernel, n_fox_tiles=nf)
    return pl.pallas_call(
        kern,
        grid=(T // tm, nf + nd),
        in_specs=[
            pl.BlockSpec((tm, D), lambda i, j: (i, 0)),
            pl.BlockSpec((1, D), lambda i, j: (0, 0)),
            pl.BlockSpec((D, tn), lambda i, j: (0, j)),
            pl.BlockSpec((D, LANES), lambda i, j: (0, 0)),
            pl.BlockSpec((1, LANES), lambda i, j: (0, 0)),
        ],
        out_specs=[
            pl.BlockSpec((tm, tn), lambda i, j: (i, jnp.minimum(j, nf - 1))),
            pl.BlockSpec((tm, tn), lambda i, j: (i, jnp.maximum(j - nf, 0))),
            pl.BlockSpec((tm, LANES), lambda i, j: (i, 0)),
        ],
        out_shape=[
            jax.ShapeDtypeStruct((T, n_fox), BF16),
            jax.ShapeDtypeStruct((T, n_dil), F32),
            jax.ShapeDtypeStruct((T, LANES), F32),
        ],
        scratch_shapes=[pltpu.VMEM((tm, D), BF16)],
        compiler_params=_cparams(("parallel", "arbitrary")),
        name="in_proj",
    )(x, g, w_qkv, w_f, fb)


def _split3(x):
    hi = x.astype(BF16)
    r1 = x - hi.astype(F32)
    mid = r1.astype(BF16)
    lo = (r1 - mid.astype(F32)).astype(BF16)
    return hi, mid, lo


def _cumsum_kernel(logf_ref, cs_ref, *, chunk):
    S = logf_ref.shape[0]
    n_chunks = S // chunk
    row = lax.broadcasted_iota(jnp.int32, (chunk, chunk), 0)
    col = lax.broadcasted_iota(jnp.int32, (chunk, chunk), 1)
    tri = (col <= row).astype(BF16)
    carry = jnp.zeros((1, LANES), F32)
    for ci in range(n_chunks):
        seg = logf_ref[ci * chunk:(ci + 1) * chunk, :]
        hi, mid, lo = _split3(seg)
        within = (jnp.dot(tri, hi, preferred_element_type=F32)
                  + jnp.dot(tri, mid, preferred_element_type=F32)
                  + jnp.dot(tri, lo, preferred_element_type=F32))
        within = within + carry
        cs_ref[ci * chunk:(ci + 1) * chunk, :] = within
        carry = within[chunk - 1:chunk, :]


def _fox_cumsum(logf, B, S):
    kern = functools.partial(_cumsum_kernel, chunk=128)
    return pl.pallas_call(
        kern,
        grid=(B,),
        in_specs=[pl.BlockSpec((S, LANES), lambda b: (b, 0))],
        out_specs=pl.BlockSpec((S, LANES), lambda b: (b, 0)),
        out_shape=jax.ShapeDtypeStruct((B * S, LANES), F32),
        compiler_params=_cparams(("parallel",)),
        name="fox_cumsum",
    )(logf)


def _fox_kernel(q_ref, k_ref, v_ref, cs_ref, o_ref, kaug_ref, vt_ref, qaug_ref, s0_ref, s1_ref,
                acc_ref, *, tq, scale):
    h = pl.program_id(1)
    qi = pl.program_id(2)
    S = k_ref.shape[0]
    tk = tq // 2
    scale2 = scale * LOG2E
    nt = (((1,), (1,)), ((), ()))

    @pl.when(qi == 0)
    def _():
        lane = lax.broadcasted_iota(jnp.int32, (tq, LANES), 1)

        def setup(ci, carry):
            rows = pl.ds(pl.multiple_of(ci * tq, tq), tq)
            cs = cs_ref[rows, :]
            col = jnp.sum(jnp.where(lane == h, cs, 0.0), axis=-1, keepdims=True)
            hi, mid, lo = _split3(col * (-1.0 / scale))
            aug = jnp.where(lane == 0, hi.astype(F32),
                            jnp.where(lane == 1, mid.astype(F32),
                                      jnp.where(lane == 2, lo.astype(F32), 0.0)))
            kaug_ref[rows, :HEAD_DIM] = k_ref[rows, :]
            kaug_ref[rows, HEAD_DIM:] = aug.astype(BF16)
            vt_ref[:, rows] = v_ref[rows, :].T
            return carry

        lax.fori_loop(0, S // tq, setup, 0)

    lane = lax.broadcasted_iota(jnp.int32, (tq, LANES), 1)
    qaug_ref[:, :HEAD_DIM] = q_ref[...]
    qaug_ref[:, HEAD_DIM:] = jnp.where(lane < 3, 1.0, 0.0).astype(BF16)
    acc_ref[...] = jnp.zeros(acc_ref.shape, F32)

    def scores(blk, s_ref, q_lo=0):
        rows = pl.ds(pl.multiple_of(blk * tk, tk), tk)
        s_ref[:, q_lo:] = lax.dot_general(kaug_ref[rows, :], qaug_ref[q_lo:, :], nt,
                                          preferred_element_type=F32)

    def softmax_pv(blk, s_ref, m, l, diag=None):
        q_lo = tk if diag == 1 else 0
        st = s_ref[:, q_lo:]
        if diag is not None:
            row = lax.broadcasted_iota(jnp.int32, (tk, tk), 0)
            col = lax.broadcasted_iota(jnp.int32, (tk, tk), 1)
            tri = jnp.where(row <= col, st[:, :tk], NEG_INF)
            st = tri if diag == 1 else jnp.concatenate([tri, st[:, tk:]], axis=1)
        mq, lq = m[:, q_lo:], l[:, q_lo:]
        m_new = jnp.maximum(mq, jnp.max(st, axis=0, keepdims=True) * scale2)
        alpha = jnp.exp2(mq - m_new)
        p = jnp.exp2(st * scale2 - m_new)
        l_new = alpha * lq + jnp.sum(p, axis=0, keepdims=True)
        rows = pl.ds(pl.multiple_of(blk * tk, tk), tk)
        pv = jnp.dot(vt_ref[:, rows], p.astype(BF16), preferred_element_type=F32)
        acc_ref[:, q_lo:] = alpha * acc_ref[:, q_lo:] + pv
        if q_lo:
            m_new = jnp.concatenate([m[:, :q_lo], m_new], axis=1)
            l_new = jnp.concatenate([l[:, :q_lo], l_new], axis=1)
        return m_new, l_new

    def pair(t, carry):
        m, l = carry
        j = 2 * t
        scores(j + 1, s1_ref)
        m, l = softmax_pv(j, s0_ref, m, l)
        scores(j + 2, s0_ref)
        m, l = softmax_pv(j + 1, s1_ref, m, l)
        return m, l

    m = jnp.full((1, tq), NEG_INF, F32)
    l = jnp.zeros((1, tq), F32)
    scores(0, s0_ref)
    m, l = lax.fori_loop(0, qi, pair, (m, l))
    d0 = 2 * qi
    scores(d0 + 1, s1_ref, q_lo=tk)
    m, l = softmax_pv(d0, s0_ref, m, l, diag=0)
    m, l = softmax_pv(d0 + 1, s1_ref, m, l, diag=1)
    o_ref[...] = (acc_ref[...] / l).T


def _fox_attention(fox, cs, B, S, *, tq=512):
    H = N_HEADS_FOX
    kern = functools.partial(_fox_kernel, tq=tq, scale=HEAD_DIM ** -0.5)
    return pl.pallas_call(
        kern,
        grid=(B, H, S // tq),
        in_specs=[
            pl.BlockSpec((None, tq, HEAD_DIM), lambda b, h, i: (b, i, h)),
            pl.BlockSpec((None, S, HEAD_DIM), lambda b, h, i: (b, 0, H + h)),
            pl.BlockSpec((None, S, HEAD_DIM), lambda b, h, i: (b, 0, 2 * H + h)),
            pl.BlockSpec((None, S, LANES), lambda b, h, i: (b, 0, 0)),
        ],
        out_specs=pl.BlockSpec((None, tq, HEAD_DIM), lambda b, h, i: (b, i, h)),
        out_shape=jax.ShapeDtypeStruct((B, S, D_FOX), F32),
        scratch_shapes=[
            pltpu.VMEM((S, 2 * HEAD_DIM), BF16),
            pltpu.VMEM((HEAD_DIM, S), BF16),
            pltpu.VMEM((tq, 2 * HEAD_DIM), BF16),
            pltpu.VMEM((tq // 2, tq), F32),
            pltpu.VMEM((tq // 2, tq), F32),
            pltpu.VMEM((HEAD_DIM, tq), F32),
        ],
        compiler_params=_cparams(("parallel", "parallel", "arbitrary")),
        name="fox_attention",
    )(fox, fox, fox, cs)


def _rel_bucket_np(dist):
    max_exact = REL_BUCKETS // 2
    d = np.maximum(dist.astype(np.float32), np.float32(1.0))
    ratio = np.log(d / np.float32(max_exact)).astype(np.float32) / np.float32(
        math.log(REL_MAX_DISTANCE / max_exact))
    large = max_exact + (ratio * np.float32(REL_BUCKETS - max_exact)).astype(np.int32)
    large = np.minimum(large, REL_BUCKETS - 1)
    return np.where(dist < max_exact, dist, large)


def _block_perm(dilation):
    n_chunks = MAX_DIL // dilation
    rows = DIL_BLOCK // n_chunks
    a = np.arange(n_chunks)[:, None]
    nn = np.arange(rows)[None, :]
    return (nn * n_chunks + a).reshape(-1)


def _bucket_tables():
    tabs = []
    for window, dilation in DIL_PATTERNS:
        span = window // dilation
        perm = _block_perm(dilation)
        i = perm[:, None]
        j = np.concatenate([perm, DIL_BLOCK + perm])[None, :]
        rel = DIL_BLOCK + i - j
        in_band = (rel >= 0) & (rel <= span)
        bucket = _rel_bucket_np(np.clip(rel, 0, span) * dilation)
        tabs.append(np.where(in_band, bucket, -1).astype(np.int32))
    return np.stack(tabs)


def _bias_table_kernel(rel_bias_ref, idx_ref, tab_ref):
    h = pl.program_id(1)
    idx = idx_ref[...]
    tab = jnp.full(idx.shape, NEG_INF, F32)
    for b in range(REL_BUCKETS):
        tab = jnp.where(idx == b, rel_bias_ref[b, h], tab)
    tab_ref[...] = tab


def _bias_tables(rel_bias):
    idx = jnp.asarray(_bucket_tables())
    P = len(DIL_PATTERNS)
    return pl.pallas_call(
        _bias_table_kernel,
        grid=(P, N_HEADS_DIL),
        in_specs=[
            pl.BlockSpec(memory_space=pltpu.SMEM),
            pl.BlockSpec((None, DIL_BLOCK, 2 * DIL_BLOCK), lambda p, h: (p, 0, 0)),
        ],
        out_specs=pl.BlockSpec((None, None, DIL_BLOCK, 2 * DIL_BLOCK), lambda p, h: (p, h, 0, 0)),
        out_shape=jax.ShapeDtypeStruct((P, N_HEADS_DIL, DIL_BLOCK, 2 * DIL_BLOCK), F32),
        compiler_params=_cparams(("parallel", "parallel")),
        name="dil_bias_tables",
    )(rel_bias, idx)


def _dil_kernel(q_ref, k_ref, v_ref, tab_ref, o_ref, m_ref, l_ref, acc_ref, *, scale):
    n_sub = q_ref.shape[1]

    def gather(ref, dilation, r0, blk, n_blocks=1):
        n_chunks = MAX_DIL // dilation
        rows = DIL_BLOCK // n_chunks
        parts = []
        for b in range(n_blocks):
            start = pl.multiple_of((blk + b) * rows, rows)
            for a in range(n_chunks):
                parts.append(ref[r0 + a * dilation, pl.ds(start, rows), :])
        return parts[0] if len(parts) == 1 else jnp.concatenate(parts, axis=0)

    def scatter(ref, val, dilation, r0, blk):
        n_chunks = MAX_DIL // dilation
        rows = DIL_BLOCK // n_chunks
        start = pl.multiple_of(blk * rows, rows)
        for a in range(n_chunks):
            ref[r0 + a * dilation, pl.ds(start, rows), :] = val[a * rows:(a + 1) * rows]

    def block(p_idx, dilation, r0, blk, first_block, first_pattern, last_pattern):
        q = gather(q_ref, dilation, r0, blk).astype(BF16)
        if first_block:
            k = gather(k_ref, dilation, r0, blk).astype(BF16)
            v = gather(v_ref, dilation, r0, blk).astype(BF16)
            tab = tab_ref[p_idx, :, DIL_BLOCK:]
        else:
            k = gather(k_ref, dilation, r0, blk - 1, 2).astype(BF16)
            v = gather(v_ref, dilation, r0, blk - 1, 2).astype(BF16)
            tab = tab_ref[p_idx]
        s = lax.dot_general(q, k, (((1,), (1,)), ((), ())), preferred_element_type=F32)
        s = s * scale + tab
        m1 = jnp.max(s, axis=-1, keepdims=True)
        p = jnp.exp(s - m1)
        l1 = jnp.sum(p, axis=-1, keepdims=True)
        o1 = jnp.dot(p.astype(BF16), v, preferred_element_type=F32)
        if first_pattern:
            m, l, acc = m1, l1, o1
        else:
            m0 = gather(m_ref, dilation, r0, blk)
            l0 = gather(l_ref, dilation, r0, blk)
            acc0 = gather(acc_ref, dilation, r0, blk)
            m = jnp.maximum(m0, m1)
            a0 = jnp.exp(m0 - m)
            a1 = jnp.exp(m1 - m)
            l = a0 * l0 + a1 * l1
            acc = a0 * acc0 + a1 * o1
        if last_pattern:
            scatter(o_ref, acc / l, dilation, r0, blk)
        else:
            scatter(m_ref, m, dilation, r0, blk)
            scatter(l_ref, l, dilation, r0, blk)
            scatter(acc_ref, acc, dilation, r0, blk)

    order = sorted(range(len(DIL_PATTERNS)), key=lambda i: -DIL_PATTERNS[i][1])
    for pos, p_idx in enumerate(order):
        dilation = DIL_PATTERNS[p_idx][1]
        n_blocks = (n_sub * MAX_DIL // dilation) // DIL_BLOCK
        first_pattern = pos == 0
        last_pattern = pos == len(order) - 1
        for r0 in range(dilation):
            block(p_idx, dilation, r0, 0, True, first_pattern, last_pattern)

            def body(blk, carry, r0=r0, p_idx=p_idx, dilation=dilation,
                     first_pattern=first_pattern, last_pattern=last_pattern):
                block(p_idx, dilation, r0, blk, False, first_pattern, last_pattern)
                return carry

            lax.fori_loop(1, n_blocks, body, 0)


def _dil_attention(qkv, tabs, B, S):
    H = N_HEADS_DIL
    n_sub = S // MAX_DIL
    P = len(DIL_PATTERNS)
    kern = functools.partial(_dil_kernel, scale=HEAD_DIM ** -0.5)
    blk = (None, MAX_DIL, n_sub, HEAD_DIM)
    return pl.pallas_call(
        kern,
        grid=(B, H),
        in_specs=[
            pl.BlockSpec(blk, lambda b, h: (b, 0, 0, h)),
            pl.BlockSpec(blk, lambda b, h: (b, 0, 0, H + h)),
            pl.BlockSpec(blk, lambda b, h: (b, 0, 0, 2 * H + h)),
            pl.BlockSpec((P, None, DIL_BLOCK, 2 * DIL_BLOCK), lambda b, h: (0, h, 0, 0)),
        ],
        out_specs=pl.BlockSpec(blk, lambda b, h: (b, 0, 0, h)),
        out_shape=jax.ShapeDtypeStruct((B, MAX_DIL, n_sub, D_DIL), F32),
        scratch_shapes=[
            pltpu.VMEM((MAX_DIL, n_sub, 1), F32),
            pltpu.VMEM((MAX_DIL, n_sub, 1), F32),
            pltpu.VMEM((MAX_DIL, n_sub, HEAD_DIM), F32),
        ],
        compiler_params=_cparams(("parallel", "parallel")),
        name="dil_attention",
    )(qkv, qkv, qkv, tabs)


def _out_proj_kernel(ya_ref, yb_ref, ga_ref, gb_ref, w_ref, x_ref, o_ref, mix_ref):
    j = pl.program_id(1)

    @pl.when(j == 0)
    def _():
        ya = ya_ref[...]
        yb = yb_ref[...]
        mix_ref[:, :D_FOX] = (ya * _rms_scale(ya) * ga_ref[...]).astype(BF16)
        mix_ref[:, D_FOX:] = (yb * _rms_scale(yb) * gb_ref[...]).astype(BF16)

    o_ref[...] = x_ref[...] + jnp.dot(mix_ref[...], w_ref[...], preferred_element_type=F32)


def _out_proj(ya, yb, ga, gb, w, x, *, tm=512, tn=512):
    T, D = x.shape
    return pl.pallas_call(
        _out_proj_kernel,
        grid=(T // tm, D // tn),
        in_specs=[
            pl.BlockSpec((tm, D_FOX), lambda i, j: (i, 0)),
            pl.BlockSpec((tm, D_DIL), lambda i, j: (i, 0)),
            pl.BlockSpec((1, D_FOX), lambda i, j: (0, 0)),
            pl.BlockSpec((1, D_DIL), lambda i, j: (0, 0)),
            pl.BlockSpec((D_FOX + D_DIL, tn), lambda i, j: (0, j)),
            pl.BlockSpec((tm, tn), lambda i, j: (i, j)),
        ],
        out_specs=pl.BlockSpec((tm, tn), lambda i, j: (i, j)),
        out_shape=jax.ShapeDtypeStruct((T, D), F32),
        scratch_shapes=[pltpu.VMEM((tm, D_FOX + D_DIL), BF16)],
        compiler_params=_cparams(("parallel", "arbitrary")),
        name="out_proj",
    )(ya, yb, ga, gb, w, x)


def _mlp_kernel(x_ref, g_ref, w1_ref, w2_ref, fg_ref, o_ref, xn_ref, acc_ref, *, final_norm):
    f = pl.program_id(1)

    @pl.when(f == 0)
    def _():
        x = x_ref[...]
        xn_ref[...] = (x * _rms_scale(x) * g_ref[...]).astype(BF16)
        acc_ref[...] = x

    h = jnp.dot(xn_ref[...], w1_ref[...], preferred_element_type=F32)
    h = jnp.square(jnp.maximum(h, 0.0))
    acc_ref[...] += jnp.dot(h.astype(BF16), w2_ref[...], preferred_element_type=F32)

    @pl.when(f == pl.num_programs(1) - 1)
    def _():
        y = acc_ref[...]
        if final_norm:
            y = y * _rms_scale(y) * fg_ref[...]
        o_ref[...] = y


def _mlp(x, g, w1, w2, fg, *, final_norm, tm=512, tf=512):
    T, D = x.shape
    F = w1.shape[1]
    kern = functools.partial(_mlp_kernel, final_norm=final_norm)
    return pl.pallas_call(
        kern,
        grid=(T // tm, F // tf),
        in_specs=[
            pl.BlockSpec((tm, D), lambda i, f: (i, 0)),
            pl.BlockSpec((1, D), lambda i, f: (0, 0)),
            pl.BlockSpec((D, tf), lambda i, f: (0, f)),
            pl.BlockSpec((tf, D), lambda i, f: (f, 0)),
            pl.BlockSpec((1, D), lambda i, f: (0, 0)),
        ],
        out_specs=pl.BlockSpec((tm, D), lambda i, f: (i, 0)),
        out_shape=jax.ShapeDtypeStruct((T, D), F32),
        scratch_shapes=[pltpu.VMEM((tm, D), BF16), pltpu.VMEM((tm, D), F32)],
        compiler_params=_cparams(("parallel", "arbitrary")),
        name="mlp",
    )(x, g, w1, w2, fg)


def kernel(x, norm1_g, w_in, forget_b, rel_bias, outnorm_a_g, outnorm_b_g, w_out,
           norm2_g, w_mlp_in, w_mlp_out, final_norm_g):
    B, S, D = x.shape
    T = B * S
    depth = w_in.shape[0]
    n_sub = S // MAX_DIL
    assert S % (MAX_DIL * DIL_BLOCK) == 0 and w_in.shape[2] == 3 * D_FOX + N_HEADS_FOX + 3 * D_DIL

    tabs = _bias_tables(rel_bias.astype(F32))
    fg = final_norm_g.reshape(1, D).astype(F32)
    xf = x.reshape(T, D)
    for l in range(depth):
        f0 = 3 * D_FOX
        w_qkv = jnp.concatenate([w_in[l, :, :f0], w_in[l, :, f0 + N_HEADS_FOX:]], axis=1).astype(BF16)
        w_f = jnp.pad(w_in[l, :, f0:f0 + N_HEADS_FOX], ((0, 0), (0, LANES - N_HEADS_FOX))).astype(BF16)
        fb = jnp.pad(forget_b[l].astype(F32), (0, LANES - N_HEADS_FOX)).reshape(1, LANES)

        fox, dil, logf = _in_proj(xf, norm1_g[l].reshape(1, D), w_qkv, w_f, fb)
        cs = _fox_cumsum(logf, B, S)
        ya = _fox_attention(fox.reshape(B, S, 3 * D_FOX), cs.reshape(B, S, LANES), B, S)

        dil = dil.reshape(B, n_sub, MAX_DIL, 3 * D_DIL).transpose(0, 2, 1, 3)
        yb = _dil_attention(dil, tabs, B, S)
        yb = yb.transpose(0, 2, 1, 3).reshape(T, D_DIL)

        xf = _out_proj(ya.reshape(T, D_FOX), yb, outnorm_a_g[l].reshape(1, D_FOX),
                       outnorm_b_g[l].reshape(1, D_DIL), w_out[l].astype(BF16), xf)
        xf = _mlp(xf, norm2_g[l].reshape(1, D), w_mlp_in[l].astype(BF16), w_mlp_out[l].astype(BF16),
                  fg, final_norm=(l == depth - 1))
    return xf.reshape(B, S, D)
```

```python
import functools
import math

import numpy as np
import jax
import jax.numpy as jnp
from jax import lax
from jax.experimental import pallas as pl
from jax.experimental.pallas import tpu as pltpu

F32 = jnp.float32
BF16 = jnp.bfloat16

HEAD_DIM = 128
N_HEADS_FOX = 8
N_HEADS_DIL = 8
D_FOX = N_HEADS_FOX * HEAD_DIM
D_DIL = N_HEADS_DIL * HEAD_DIM
DIL_PATTERNS = ((128, 1), (512, 4), (2048, 16))
DIL_BLOCK = 128
MAX_DIL = 16
REL_BUCKETS = 32
REL_MAX_DISTANCE = 2048
NORM_EPS = 1e-6
NEG_INF = -1e30
LOG2E = math.log2(math.e)

LANES = 128
VMEM_LIMIT = 56 * 1024 * 1024


def _cparams(sem):
    return pltpu.CompilerParams(dimension_semantics=sem, vmem_limit_bytes=VMEM_LIMIT)


def _rms_scale(x):
    return lax.rsqrt(jnp.mean(x * x, axis=-1, keepdims=True) + NORM_EPS)


def _in_proj_kernel(x_ref, g_ref, w_ref, wf_ref, fb_ref, fox_ref, dil_ref, logf_ref, xn_ref,
                    *, n_fox_tiles):
    j = pl.program_id(1)

    @pl.when(j == 0)
    def _():
        x = x_ref[...]
        xn = (x * _rms_scale(x) * g_ref[...]).astype(BF16)
        xn_ref[...] = xn
        f = jnp.dot(xn, wf_ref[...], preferred_element_type=F32) + fb_ref[...]
        logf_ref[...] = jnp.minimum(f, 0.0) - jnp.log1p(jnp.exp(-jnp.abs(f)))

    acc = jnp.dot(xn_ref[...], w_ref[...], preferred_element_type=F32)

    @pl.when(j < n_fox_tiles)
    def _():
        fox_ref[...] = acc.astype(BF16)

    @pl.when(j >= n_fox_tiles)
    def _():
        dil_ref[...] = acc


def _in_proj(x, g, w_qkv, w_f, fb, *, tm=512, tn=512):
    T, D = x.shape
    n_fox = 3 * D_FOX
    n_dil = 3 * D_DIL
    nf, nd = n_fox // tn, n_dil // tn
    kern = functools.partial(_in_proj_kernel, n_fox_tiles=nf)
    return pl.pallas_call(
        kern,
        grid=(T // tm, nf + nd),
        in_specs=[
            pl.BlockSpec((tm, D), lambda i, j: (i, 0)),
            pl.BlockSpec((1, D), lambda i, j: (0, 0)),
            pl.BlockSpec((D, tn), lambda i, j: (0, j)),
            pl.BlockSpec((D, LANES), lambda i, j: (0, 0)),
            pl.BlockSpec((1, LANES), lambda i, j: (0, 0)),
        ],
        out_specs=[
            pl.BlockSpec((tm, tn), lambda i, j: (i, jnp.minimum(j, nf - 1))),
            pl.BlockSpec((tm, tn), lambda i, j: (i, jnp.maximum(j - nf, 0))),
            pl.BlockSpec((tm, LANES), lambda i, j: (i, 0)),
        ],
        out_shape=[
            jax.ShapeDtypeStruct((T, n_fox), BF16),
            jax.ShapeDtypeStruct((T, n_dil), F32),
            jax.ShapeDtypeStruct((T, LANES), F32),
        ],
        scratch_shapes=[pltpu.VMEM((tm, D), BF16)],
        compiler_params=_cparams(("parallel", "arbitrary")),
        name="in_proj",
    )(x, g, w_qkv, w_f, fb)


def _split3(x):
    hi = x.astype(BF16)
    r1 = x - hi.astype(F32)
    mid = r1.astype(BF16)
    lo = (r1 - mid.astype(F32)).astype(BF16)
    return hi, mid, lo


def _cumsum_kernel(logf_ref, cs_ref, *, chunk):
    S = logf_ref.shape[0]
    n_chunks = S // chunk
    row = lax.broadcasted_iota(jnp.int32, (chunk, chunk), 0)
    col = lax.broadcasted_iota(jnp.int32, (chunk, chunk), 1)
    tri = (col <= row).astype(BF16)
    carry = jnp.zeros((1, LANES), F32)
    for ci in range(n_chunks):
        seg = logf_ref[ci * chunk:(ci + 1) * chunk, :]
        hi, mid, lo = _split3(seg)
        within = (jnp.dot(tri, hi, preferred_element_type=F32)
                  + jnp.dot(tri, mid, preferred_element_type=F32)
                  + jnp.dot(tri, lo, preferred_element_type=F32))
        within = within + carry
        cs_ref[ci * chunk:(ci + 1) * chunk, :] = within
        carry = within[chunk - 1:chunk, :]


def _fox_cumsum(logf, B, S):
    kern = functools.partial(_cumsum_kernel, chunk=128)
    return pl.pallas_call(
        kern,
        grid=(B,),
        in_specs=[pl.BlockSpec((S, LANES), lambda b: (b, 0))],
        out_specs=pl.BlockSpec((S, LANES), lambda b: (b, 0)),
        out_shape=jax.ShapeDtypeStruct((B * S, LANES), F32),
        compiler_params=_cparams(("parallel",)),
        name="fox_cumsum",
    )(logf)


def _fox_kernel(q_ref, k_ref, v_ref, cs_ref, o_ref, kaug_ref, vt_ref, qaug_ref, s0_ref, s1_ref,
                acc_ref, *, tq, scale):
    h = pl.program_id(1)
    qi = pl.program_id(2)
    S = k_ref.shape[0]
    tk = tq // 2
    scale2 = scale * LOG2E
    nt = (((1,), (1,)), ((), ()))

    @pl.when(qi == 0)
    def _():
        lane = lax.broadcasted_iota(jnp.int32, (tq, LANES), 1)

        def setup(ci, carry):
            rows = pl.ds(pl.multiple_of(ci * tq, tq), tq)
            cs = cs_ref[rows, :]
            col = jnp.sum(jnp.where(lane == h, cs, 0.0), axis=-1, keepdims=True)
            hi, mid, lo = _split3(col * (-1.0 / scale))
            aug = jnp.where(lane == 0, hi.astype(F32),
                            jnp.where(lane == 1, mid.astype(F32),
                                      jnp.where(lane == 2, lo.astype(F32), 0.0)))
            kaug_ref[rows, :HEAD_DIM] = k_ref[rows, :]
            kaug_ref[rows, HEAD_DIM:] = aug.astype(BF16)
            vt_ref[:, rows] = v_ref[rows, :].T
            return carry

        lax.fori_loop(0, S // tq, setup, 0)

    lane = lax.broadcasted_iota(jnp.int32, (tq, LANES), 1)
    qaug_ref[:, :HEAD_DIM] = q_ref[...]
    qaug_ref[:, HEAD_DIM:] = jnp.where(lane < 3, 1.0, 0.0).astype(BF16)
    acc_ref[...] = jnp.zeros(acc_ref.shape, F32)

    def scores(blk, s_ref, q_lo=0):
        rows = pl.ds(pl.multiple_of(blk * tk, tk), tk)
        s_ref[:, q_lo:] = lax.dot_general(kaug_ref[rows, :], qaug_ref[q_lo:, :], nt,
                                          preferred_element_type=F32)

    def softmax_pv(blk, s_ref, m, l, diag=None):
        q_lo = tk if diag == 1 else 0
        st = s_ref[:, q_lo:]
        if diag is not None:
            row = lax.broadcasted_iota(jnp.int32, (tk, tk), 0)
            col = lax.broadcasted_iota(jnp.int32, (tk, tk), 1)
            tri = jnp.where(row <= col, st[:, :tk], NEG_INF)
            st = tri if diag == 1 else jnp.concatenate([tri, st[:, tk:]], axis=1)
        mq, lq = m[:, q_lo:], l[:, q_lo:]
        m_new = jnp.maximum(mq, jnp.max(st, axis=0, keepdims=True) * scale2)
        alpha = jnp.exp2(mq - m_new)
        p = jnp.exp2(st * scale2 - m_new)
        l_new = alpha * lq + jnp.sum(p, axis=0, keepdims=True)
        rows = pl.ds(pl.multiple_of(blk * tk, tk), tk)
        pv = jnp.dot(vt_ref[:, rows], p.astype(BF16), preferred_element_type=F32)
        acc_ref[:, q_lo:] = alpha * acc_ref[:, q_lo:] + pv
        if q_lo:
            m_new = jnp.concatenate([m[:, :q_lo], m_new], axis=1)
            l_new = jnp.concatenate([l[:, :q_lo], l_new], axis=1)
        return m_new, l_new

    def pair(t, carry):
        m, l = carry
        j = 2 * t
        scores(j + 1, s1_ref)
        m, l = softmax_pv(j, s0_ref, m, l)
        scores(j + 2, s0_ref)
        m, l = softmax_pv(j + 1, s1_ref, m, l)
        return m, l

    m = jnp.full((1, tq), NEG_INF, F32)
    l = jnp.zeros((1, tq), F32)
    scores(0, s0_ref)
    m, l = lax.fori_loop(0, qi, pair, (m, l))
    d0 = 2 * qi
    scores(d0 + 1, s1_ref, q_lo=tk)
    m, l = softmax_pv(d0, s0_ref, m, l, diag=0)
    m, l = softmax_pv(d0 + 1, s1_ref, m, l, diag=1)
    o_ref[...] = (acc_ref[...] / l).T


def _fox_attention(fox, cs, B, S, *, tq=512):
    H = N_HEADS_FOX
    kern = functools.partial(_fox_kernel, tq=tq, scale=HEAD_DIM ** -0.5)
    return pl.pallas_call(
        kern,
        grid=(B, H, S // tq),
        in_specs=[
            pl.BlockSpec((None, tq, HEAD_DIM), lambda b, h, i: (b, i, h)),
            pl.BlockSpec((None, S, HEAD_DIM), lambda b, h, i: (b, 0, H + h)),
            pl.BlockSpec((None, S, HEAD_DIM), lambda b, h, i: (b, 0, 2 * H + h)),
            pl.BlockSpec((None, S, LANES), lambda b, h, i: (b, 0, 0)),
        ],
        out_specs=pl.BlockSpec((None, tq, HEAD_DIM), lambda b, h, i: (b, i, h)),
        out_shape=jax.ShapeDtypeStruct((B, S, D_FOX), F32),
        scratch_shapes=[
            pltpu.VMEM((S, 2 * HEAD_DIM), BF16),
            pltpu.VMEM((HEAD_DIM, S), BF16),
            pltpu.VMEM((tq, 2 * HEAD_DIM), BF16),
            pltpu.VMEM((tq // 2, tq), F32),
            pltpu.VMEM((tq // 2, tq), F32),
            pltpu.VMEM((HEAD_DIM, tq), F32),
        ],
        compiler_params=_cparams(("parallel", "parallel", "arbitrary")),
        name="fox_attention",
    )(fox, fox, fox, cs)


def _rel_bucket_np(dist):
    max_exact = REL_BUCKETS // 2
    d = np.maximum(dist.astype(np.float32), np.float32(1.0))
    ratio = np.log(d / np.float32(max_exact)).astype(np.float32) / np.float32(
        math.log(REL_MAX_DISTANCE / max_exact))
    large = max_exact + (ratio * np.float32(REL_BUCKETS - max_exact)).astype(np.int32)
    large = np.minimum(large, REL_BUCKETS - 1)
    return np.where(dist < max_exact, dist, large)


def _block_perm(dilation):
    n_chunks = MAX_DIL // dilation
    rows = DIL_BLOCK // n_chunks
    a = np.arange(n_chunks)[:, None]
    nn = np.arange(rows)[None, :]
    return (nn * n_chunks + a).reshape(-1)


def _bucket_tables():
    tabs = []
    for window, dilation in DIL_PATTERNS:
        span = window // dilation
        perm = _block_perm(dilation)
        i = perm[:, None]
        j = np.concatenate([perm, DIL_BLOCK + perm])[None, :]
        rel = DIL_BLOCK + i - j
        in_band = (rel >= 0) & (rel <= span)
        bucket = _rel_bucket_np(np.clip(rel, 0, span) * dilation)
        full = np.where(in_band, bucket, -1).astype(np.int32)
        first = np.where(j >= DIL_BLOCK, full, -1).astype(np.int32)
        tabs.append(np.stack([full, first]))
    return np.stack(tabs)


def _bias_table_kernel(rel_bias_ref, idx_ref, tab_ref):
    h = pl.program_id(1)
    idx = idx_ref[...]
    tab = jnp.full(idx.shape, NEG_INF, F32)
    for b in range(REL_BUCKETS):
        tab = jnp.where(idx == b, rel_bias_ref[b, h] * LOG2E, tab)
    tab_ref[...] = tab


def _bias_tables(rel_bias):
    idx = jnp.asarray(_bucket_tables())
    P = len(DIL_PATTERNS)
    blk = (2, DIL_BLOCK, 2 * DIL_BLOCK)
    return pl.pallas_call(
        _bias_table_kernel,
        grid=(P, N_HEADS_DIL),
        in_specs=[
            pl.BlockSpec(memory_space=pltpu.SMEM),
            pl.BlockSpec((None,) + blk, lambda p, h: (p, 0, 0, 0)),
        ],
        out_specs=pl.BlockSpec((None, None) + blk, lambda p, h: (p, h, 0, 0, 0)),
        out_shape=jax.ShapeDtypeStruct((P, N_HEADS_DIL) + blk, F32),
        compiler_params=_cparams(("parallel", "parallel")),
        name="dil_bias_tables",
    )(rel_bias, idx)


def _dil_kernel(q_ref, k_ref, v_ref, tab_ref, o_ref, op_ref, lse_ref, *, scale, group):
    n_sub = q_ref.shape[1]
    n_pat = len(DIL_PATTERNS)
    scale2 = scale * LOG2E
    ones = jnp.ones((2 * DIL_BLOCK, HEAD_DIM), BF16)

    def rows_at(start, rows):
        return pl.ds(start if isinstance(start, int) else pl.multiple_of(start, rows), rows)

    def gather(ref, dilation, r0, starts):
        n_chunks = MAX_DIL // dilation
        rows = DIL_BLOCK // n_chunks
        parts = [ref[r0 + a * dilation, rows_at(s, rows), :]
                 for s in starts for a in range(n_chunks)]
        return parts[0] if len(parts) == 1 else jnp.concatenate(parts, axis=0)

    def scatter(ref, val, dilation, r0, start):
        n_chunks = MAX_DIL // dilation
        rows = DIL_BLOCK // n_chunks
        for a in range(n_chunks):
            ref[r0 + a * dilation, rows_at(start, rows), :] = val[a * rows:(a + 1) * rows]

    def block(p_idx, dilation, r0, blk):
        rows = DIL_BLOCK * dilation // MAX_DIL
        cur = blk * rows
        q = gather(q_ref, dilation, r0, [cur]).astype(BF16)
        if isinstance(blk, int) and blk == 0:
            starts = [cur]
            tab = tab_ref[p_idx, 1, :, DIL_BLOCK:]
        elif isinstance(blk, int):
            starts = [cur - rows, cur]
            tab = tab_ref[p_idx, 0]
        else:
            starts = [jnp.maximum(blk - 1, 0) * rows, cur]
            tab = tab_ref[p_idx, (blk == 0).astype(jnp.int32)]
        k = gather(k_ref, dilation, r0, starts).astype(BF16)
        v = gather(v_ref, dilation, r0, starts).astype(BF16)
        s2 = lax.dot_general(q, k, (((1,), (1,)), ((), ())), preferred_element_type=F32)
        s2 = s2 * scale2 + tab
        m = jnp.max(s2, axis=-1, keepdims=True)
        p = jnp.exp2(s2 - m).astype(BF16)
        ol = jnp.dot(p, jnp.concatenate([v, ones[:v.shape[0]]], axis=1), preferred_element_type=F32)
        l = ol[:, HEAD_DIM:]
        scatter(op_ref.at[p_idx], ol[:, :HEAD_DIM] / l, dilation, r0, cur)
        scatter(lse_ref.at[p_idx], m + jnp.log2(l), dilation, r0, cur)

    for p_idx, (_, dilation) in enumerate(DIL_PATTERNS):
        n_blocks = (n_sub * MAX_DIL // dilation) // DIL_BLOCK
        if n_blocks <= 2:
            per_iter = group // n_blocks

            def body(g, carry, p_idx=p_idx, dilation=dilation, n_blocks=n_blocks, per_iter=per_iter):
                for i in range(per_iter):
                    for blk in range(n_blocks):
                        block(p_idx, dilation, g * per_iter + i, blk)
                return carry

            lax.fori_loop(0, dilation // per_iter, body, 0)
        else:
            per_iter = max(group // dilation, 1)

            def body(g, carry, p_idx=p_idx, dilation=dilation, per_iter=per_iter):
                for i in range(per_iter):
                    for r0 in range(dilation):
                        block(p_idx, dilation, r0, g * per_iter + i)
                return carry

            lax.fori_loop(0, n_blocks // per_iter, body, 0)

    chunk = DIL_BLOCK

    def combine(t, carry):
        r = t // (n_sub // chunk)
        rows = pl.ds(pl.multiple_of((t % (n_sub // chunk)) * chunk, chunk), chunk)
        lse = [lse_ref[p, r, rows, :] for p in range(n_pat)]
        top = functools.reduce(jnp.maximum, lse)
        w = [jnp.exp2(x - top) for x in lse]
        num = functools.reduce(lambda a, b: a + b, [w[p] * op_ref[p, r, rows, :] for p in range(n_pat)])
        o_ref[r, rows, :] = num / functools.reduce(lambda a, b: a + b, w)
        return carry

    lax.fori_loop(0, MAX_DIL * (n_sub // chunk), combine, 0)


def _dil_attention(qkv, tabs, B, S, *, group=8):
    H = N_HEADS_DIL
    n_sub = S // MAX_DIL
    P = len(DIL_PATTERNS)
    kern = functools.partial(_dil_kernel, scale=HEAD_DIM ** -0.5, group=group)
    blk = (None, MAX_DIL, n_sub, HEAD_DIM)
    return pl.pallas_call(
        kern,
        grid=(B, H),
        in_specs=[
            pl.BlockSpec(blk, lambda b, h: (b, 0, 0, h)),
            pl.BlockSpec(blk, lambda b, h: (b, 0, 0, H + h)),
            pl.BlockSpec(blk, lambda b, h: (b, 0, 0, 2 * H + h)),
            pl.BlockSpec((P, None, 2, DIL_BLOCK, 2 * DIL_BLOCK), lambda b, h: (0, h, 0, 0, 0)),
        ],
        out_specs=pl.BlockSpec(blk, lambda b, h: (b, 0, 0, h)),
        out_shape=jax.ShapeDtypeStruct((B, MAX_DIL, n_sub, D_DIL), F32),
        scratch_shapes=[
            pltpu.VMEM((P, MAX_DIL, n_sub, HEAD_DIM), F32),
            pltpu.VMEM((P, MAX_DIL, n_sub, HEAD_DIM), F32),
        ],
        compiler_params=_cparams(("parallel", "parallel")),
        name="dil_attention",
    )(qkv, qkv, qkv, tabs)


def _out_proj_kernel(ya_ref, yb_ref, ga_ref, gb_ref, w_ref, x_ref, o_ref, mix_ref):
    j = pl.program_id(1)

    @pl.when(j == 0)
    def _():
        ya = ya_ref[...]
        yb = yb_ref[...]
        mix_ref[:, :D_FOX] = (ya * _rms_scale(ya) * ga_ref[...]).astype(BF16)
        mix_ref[:, D_FOX:] = (yb * _rms_scale(yb) * gb_ref[...]).astype(BF16)

    o_ref[...] = x_ref[...] + jnp.dot(mix_ref[...], w_ref[...], preferred_element_type=F32)


def _out_proj(ya, yb, ga, gb, w, x, *, tm=512, tn=512):
    T, D = x.shape
    return pl.pallas_call(
        _out_proj_kernel,
        grid=(T // tm, D // tn),
        in_specs=[
            pl.BlockSpec((tm, D_FOX), lambda i, j: (i, 0)),
            pl.BlockSpec((tm, D_DIL), lambda i, j: (i, 0)),
            pl.BlockSpec((1, D_FOX), lambda i, j: (0, 0)),
            pl.BlockSpec((1, D_DIL), lambda i, j: (0, 0)),
            pl.BlockSpec((D_FOX + D_DIL, tn), lambda i, j: (0, j)),
            pl.BlockSpec((tm, tn), lambda i, j: (i, j)),
        ],
        out_specs=pl.BlockSpec((tm, tn), lambda i, j: (i, j)),
        out_shape=jax.ShapeDtypeStruct((T, D), F32),
        scratch_shapes=[pltpu.VMEM((tm, D_FOX + D_DIL), BF16)],
        compiler_params=_cparams(("parallel", "arbitrary")),
        name="out_proj",
    )(ya, yb, ga, gb, w, x)


def _mlp_kernel(x_ref, g_ref, w1_ref, w2_ref, fg_ref, o_ref, xn_ref, acc_ref, *, final_norm):
    f = pl.program_id(1)

    @pl.when(f == 0)
    def _():
        x = x_ref[...]
        xn_ref[...] = (x * _rms_scale(x) * g_ref[...]).astype(BF16)
        acc_ref[...] = x

    h = jnp.dot(xn_ref[...], w1_ref[...], preferred_element_type=F32)
    h = jnp.square(jnp.maximum(h, 0.0))
    acc_ref[...] += jnp.dot(h.astype(BF16), w2_ref[...], preferred_element_type=F32)

    @pl.when(f == pl.num_programs(1) - 1)
    def _():
        y = acc_ref[...]
        if final_norm:
            y = y * _rms_scale(y) * fg_ref[...]
        o_ref[...] = y


def _mlp(x, g, w1, w2, fg, *, final_norm, tm=512, tf=512):
    T, D = x.shape
    F = w1.shape[1]
    kern = functools.partial(_mlp_kernel, final_norm=final_norm)
    return pl.pallas_call(
        kern,
        grid=(T // tm, F // tf),
        in_specs=[
            pl.BlockSpec((tm, D), lambda i, f: (i, 0)),
            pl.BlockSpec((1, D), lambda i, f: (0, 0)),
            pl.BlockSpec((D, tf), lambda i, f: (0, f)),
            pl.BlockSpec((tf, D), lambda i, f: (f, 0)),
            pl.BlockSpec((1, D), lambda i, f: (0, 0)),
        ],
        out_specs=pl.BlockSpec((tm, D), lambda i, f: (i, 0)),
        out_shape=jax.ShapeDtypeStruct((T, D), F32),
        scratch_shapes=[pltpu.VMEM((tm, D), BF16), pltpu.VMEM((tm, D), F32)],
        compiler_params=_cparams(("parallel", "arbitrary")),
        name="mlp",
    )(x, g, w1, w2, fg)


def kernel(x, norm1_g, w_in, forget_b, rel_bias, outnorm_a_g, outnorm_b_g, w_out,
           norm2_g, w_mlp_in, w_mlp_out, final_norm_g):
    B, S, D = x.shape
    T = B * S
    depth = w_in.shape[0]
    n_sub = S // MAX_DIL
    assert S % (MAX_DIL * DIL_BLOCK) == 0 and w_in.shape[2] == 3 * D_FOX + N_HEADS_FOX + 3 * D_DIL

    tabs = _bias_tables(rel_bias.astype(F32))
    fg = final_norm_g.reshape(1, D).astype(F32)
    xf = x.reshape(T, D)
    for l in range(depth):
        f0 = 3 * D_FOX
        w_qkv = jnp.concatenate([w_in[l, :, :f0], w_in[l, :, f0 + N_HEADS_FOX:]], axis=1).astype(BF16)
        w_f = jnp.pad(w_in[l, :, f0:f0 + N_HEADS_FOX], ((0, 0), (0, LANES - N_HEADS_FOX))).astype(BF16)
        fb = jnp.pad(forget_b[l].astype(F32), (0, LANES - N_HEADS_FOX)).reshape(1, LANES)

        fox, dil, logf = _in_proj(xf, norm1_g[l].reshape(1, D), w_qkv, w_f, fb)
        cs = _fox_cumsum(logf, B, S)
        ya = _fox_attention(fox.reshape(B, S, 3 * D_FOX), cs.reshape(B, S, LANES), B, S)

        dil = dil.reshape(B, n_sub, MAX_DIL, 3 * D_DIL).transpose(0, 2, 1, 3)
        yb = _dil_attention(dil, tabs, B, S)
        yb = yb.transpose(0, 2, 1, 3).reshape(T, D_DIL)

        xf = _out_proj(ya.reshape(T, D_FOX), yb, outnorm_a_g[l].reshape(1, D_FOX),
                       outnorm_b_g[l].reshape(1, D_DIL), w_out[l].astype(BF16), xf)
        xf = _mlp(xf, norm2_g[l].reshape(1, D), w_mlp_in[l].astype(BF16), w_mlp_out[l].astype(BF16),
                  fg, final_norm=(l == depth - 1))
    return xf.reshape(B, S, D)
```

```python
import functools
import math

import numpy as np
import jax
import jax.numpy as jnp
from jax import lax
from jax.experimental import pallas as pl
from jax.experimental.pallas import tpu as pltpu

F32 = jnp.float32
BF16 = jnp.bfloat16

HEAD_DIM = 128
N_HEADS_FOX = 8
N_HEADS_DIL = 8
D_FOX = N_HEADS_FOX * HEAD_DIM
D_DIL = N_HEADS_DIL * HEAD_DIM
DIL_PATTERNS = ((128, 1), (512, 4), (2048, 16))
DIL_BLOCK = 128
MAX_DIL = 16
REGROUP = MAX_DIL * MAX_DIL
REL_BUCKETS = 32
REL_MAX_DISTANCE = 2048
NORM_EPS = 1e-6
NEG_INF = -1e30
LOG2E = math.log2(math.e)

LANES = 128
VMEM_LIMIT = 56 * 1024 * 1024


def _cparams(sem):
    return pltpu.CompilerParams(dimension_semantics=sem, vmem_limit_bytes=VMEM_LIMIT)


def _rms_scale(x):
    return lax.rsqrt(jnp.mean(x * x, axis=-1, keepdims=True) + NORM_EPS)


def _regroup_matrix():
    t = np.arange(REGROUP)
    src = (t % MAX_DIL) * MAX_DIL + t // MAX_DIL
    return jnp.asarray(np.eye(REGROUP, dtype=np.float32)[src], dtype=BF16)


def _in_proj_kernel(x_ref, g_ref, w_ref, wf_ref, fb_ref, perm_ref, fox_ref, dil_ref, logf_ref,
                    xn_ref, xnp_ref, *, n_fox_tiles):
    j = pl.program_id(1)
    tm = x_ref.shape[0]

    @pl.when(j == 0)
    def _():
        def chunk(ci, carry):
            rows = pl.ds(pl.multiple_of(ci * REGROUP, REGROUP), REGROUP)
            x = x_ref[rows, :]
            xn = (x * _rms_scale(x) * g_ref[...]).astype(BF16)
            xn_ref[rows, :] = xn
            xnp_ref[rows, :] = jnp.dot(perm_ref[...], xn, preferred_element_type=F32).astype(BF16)
            f = jnp.dot(xn, wf_ref[...], preferred_element_type=F32) + fb_ref[...]
            logf_ref[rows, :] = jnp.minimum(f, 0.0) - jnp.log1p(jnp.exp(-jnp.abs(f)))
            return carry

        lax.fori_loop(0, tm // REGROUP, chunk, 0)

    @pl.when(j < n_fox_tiles)
    def _():
        fox_ref[...] = jnp.dot(xn_ref[...], w_ref[...], preferred_element_type=F32).astype(BF16)

    @pl.when(j >= n_fox_tiles)
    def _():
        acc = jnp.dot(xnp_ref[...], w_ref[...], preferred_element_type=F32)
        per = REGROUP // MAX_DIL
        for c in range(tm // REGROUP):
            for r in range(MAX_DIL):
                lo = c * REGROUP + r * per
                dil_ref[r, c * per:(c + 1) * per, :] = acc[lo:lo + per, :]


def _in_proj(x, g, w_qkv, w_f, fb, perm, layer, B, S, *, tm=1024, tn=512):
    T, D = x.shape
    n_fox = 3 * D_FOX
    n_dil = 3 * D_DIL
    nf, nd = n_fox // tn, n_dil // tn
    per_b = S // tm
    kern = functools.partial(_in_proj_kernel, n_fox_tiles=nf)
    return pl.pallas_call(
        kern,
        grid=(T // tm, nf + nd),
        in_specs=[
            pl.BlockSpec((tm, D), lambda i, j: (i, 0)),
            pl.BlockSpec((None, 1, D), lambda i, j: (layer, 0, 0)),
            pl.BlockSpec((None, D, tn), lambda i, j: (layer, 0, j)),
            pl.BlockSpec((None, D, LANES), lambda i, j: (layer, 0, 0)),
            pl.BlockSpec((None, 1, LANES), lambda i, j: (layer, 0, 0)),
            pl.BlockSpec((REGROUP, REGROUP), lambda i, j: (0, 0)),
        ],
        out_specs=[
            pl.BlockSpec((tm, tn), lambda i, j: (i, jnp.minimum(j, nf - 1))),
            pl.BlockSpec((None, MAX_DIL, tm // MAX_DIL, tn),
                         lambda i, j: (i // per_b, 0, i % per_b, jnp.maximum(j - nf, 0))),
            pl.BlockSpec((tm, LANES), lambda i, j: (i, 0)),
        ],
        out_shape=[
            jax.ShapeDtypeStruct((T, n_fox), BF16),
            jax.ShapeDtypeStruct((B, MAX_DIL, S // MAX_DIL, n_dil), F32),
            jax.ShapeDtypeStruct((T, LANES), F32),
        ],
        scratch_shapes=[pltpu.VMEM((tm, D), BF16), pltpu.VMEM((tm, D), BF16)],
        compiler_params=_cparams(("parallel", "arbitrary")),
        name="in_proj",
    )(x, g, w_qkv, w_f, fb, perm)


def _split3(x):
    hi = x.astype(BF16)
    r1 = x - hi.astype(F32)
    mid = r1.astype(BF16)
    lo = (r1 - mid.astype(F32)).astype(BF16)
    return hi, mid, lo


def _cumsum_kernel(logf_ref, cs_ref, *, chunk):
    S = logf_ref.shape[0]
    n_chunks = S // chunk
    row = lax.broadcasted_iota(jnp.int32, (chunk, chunk), 0)
    col = lax.broadcasted_iota(jnp.int32, (chunk, chunk), 1)
    tri = (col <= row).astype(BF16)
    carry = jnp.zeros((1, LANES), F32)
    for ci in range(n_chunks):
        seg = logf_ref[ci * chunk:(ci + 1) * chunk, :]
        hi, mid, lo = _split3(seg)
        within = (jnp.dot(tri, hi, preferred_element_type=F32)
                  + jnp.dot(tri, mid, preferred_element_type=F32)
                  + jnp.dot(tri, lo, preferred_element_type=F32))
        within = within + carry
        cs_ref[ci * chunk:(ci + 1) * chunk, :] = within
        carry = within[chunk - 1:chunk, :]


def _fox_cumsum(logf, B, S):
    kern = functools.partial(_cumsum_kernel, chunk=128)
    return pl.pallas_call(
        kern,
        grid=(B,),
        in_specs=[pl.BlockSpec((S, LANES), lambda b: (b, 0))],
        out_specs=pl.BlockSpec((S, LANES), lambda b: (b, 0)),
        out_shape=jax.ShapeDtypeStruct((B * S, LANES), F32),
        compiler_params=_cparams(("parallel",)),
        name="fox_cumsum",
    )(logf)


def _fox_kernel(q_ref, k_ref, v_ref, cs_ref, o_ref, kaug_ref, vt_ref, qaug_ref, s0_ref, s1_ref,
                acc_ref, *, tq, scale):
    h = pl.program_id(1)
    qi = pl.program_id(2)
    S = k_ref.shape[0]
    tk = tq // 2
    scale2 = scale * LOG2E
    nt = (((1,), (1,)), ((), ()))

    @pl.when(qi == 0)
    def _():
        lane = lax.broadcasted_iota(jnp.int32, (tq, LANES), 1)

        def setup(ci, carry):
            rows = pl.ds(pl.multiple_of(ci * tq, tq), tq)
            cs = cs_ref[rows, :]
            col = jnp.sum(jnp.where(lane == h, cs, 0.0), axis=-1, keepdims=True)
            hi, mid, lo = _split3(col * (-1.0 / scale))
            aug = jnp.where(lane == 0, hi.astype(F32),
                            jnp.where(lane == 1, mid.astype(F32),
                                      jnp.where(lane == 2, lo.astype(F32), 0.0)))
            kaug_ref[rows, :HEAD_DIM] = k_ref[rows, :]
            kaug_ref[rows, HEAD_DIM:] = aug.astype(BF16)
            vt_ref[:, rows] = v_ref[rows, :].T
            return carry

        lax.fori_loop(0, S // tq, setup, 0)

    lane = lax.broadcasted_iota(jnp.int32, (tq, LANES), 1)
    qaug_ref[:, :HEAD_DIM] = q_ref[...]
    qaug_ref[:, HEAD_DIM:] = jnp.where(lane < 3, 1.0, 0.0).astype(BF16)
    acc_ref[...] = jnp.zeros(acc_ref.shape, F32)

    def scores(blk, s_ref, q_lo=0):
        rows = pl.ds(pl.multiple_of(blk * tk, tk), tk)
        s_ref[:, q_lo:] = lax.dot_general(kaug_ref[rows, :], qaug_ref[q_lo:, :], nt,
                                          preferred_element_type=F32)

    def softmax_pv(blk, s_ref, m, l, diag=None):
        q_lo = tk if diag == 1 else 0
        st = s_ref[:, q_lo:]
        if diag is not None:
            row = lax.broadcasted_iota(jnp.int32, (tk, tk), 0)
            col = lax.broadcasted_iota(jnp.int32, (tk, tk), 1)
            tri = jnp.where(row <= col, st[:, :tk], NEG_INF)
            st = tri if diag == 1 else jnp.concatenate([tri, st[:, tk:]], axis=1)
        mq, lq = m[:, q_lo:], l[:, q_lo:]
        m_new = jnp.maximum(mq, jnp.max(st, axis=0, keepdims=True) * scale2)
        alpha = jnp.exp2(mq - m_new)
        p = jnp.exp2(st * scale2 - m_new)
        l_new = alpha * lq + jnp.sum(p, axis=0, keepdims=True)
        rows = pl.ds(pl.multiple_of(blk * tk, tk), tk)
        pv = jnp.dot(vt_ref[:, rows], p.astype(BF16), preferred_element_type=F32)
        acc_ref[:, q_lo:] = alpha * acc_ref[:, q_lo:] + pv
        if q_lo:
            m_new = jnp.concatenate([m[:, :q_lo], m_new], axis=1)
            l_new = jnp.concatenate([l[:, :q_lo], l_new], axis=1)
        return m_new, l_new

    def pair(t, carry):
        m, l = carry
        j = 2 * t
        scores(j + 1, s1_ref)
        m, l = softmax_pv(j, s0_ref, m, l)
        scores(j + 2, s0_ref)
        m, l = softmax_pv(j + 1, s1_ref, m, l)
        return m, l

    m = jnp.full((1, tq), NEG_INF, F32)
    l = jnp.zeros((1, tq), F32)
    scores(0, s0_ref)
    m, l = lax.fori_loop(0, qi, pair, (m, l))
    d0 = 2 * qi
    scores(d0 + 1, s1_ref, q_lo=tk)
    m, l = softmax_pv(d0, s0_ref, m, l, diag=0)
    m, l = softmax_pv(d0 + 1, s1_ref, m, l, diag=1)
    o_ref[...] = (acc_ref[...] / l).T


def _fox_attention(fox, cs, B, S, *, tq=512):
    H = N_HEADS_FOX
    kern = functools.partial(_fox_kernel, tq=tq, scale=HEAD_DIM ** -0.5)
    return pl.pallas_call(
        kern,
        grid=(B, H, S // tq),
        in_specs=[
            pl.BlockSpec((None, tq, HEAD_DIM), lambda b, h, i: (b, i, h)),
            pl.BlockSpec((None, S, HEAD_DIM), lambda b, h, i: (b, 0, H + h)),
            pl.BlockSpec((None, S, HEAD_DIM), lambda b, h, i: (b, 0, 2 * H + h)),
            pl.BlockSpec((None, S, LANES), lambda b, h, i: (b, 0, 0)),
        ],
        out_specs=pl.BlockSpec((None, tq, HEAD_DIM), lambda b, h, i: (b, i, h)),
        out_shape=jax.ShapeDtypeStruct((B, S, D_FOX), F32),
        scratch_shapes=[
            pltpu.VMEM((S, 2 * HEAD_DIM), BF16),
            pltpu.VMEM((HEAD_DIM, S), BF16),
            pltpu.VMEM((tq, 2 * HEAD_DIM), BF16),
            pltpu.VMEM((tq // 2, tq), F32),
            pltpu.VMEM((tq // 2, tq), F32),
            pltpu.VMEM((HEAD_DIM, tq), F32),
        ],
        compiler_params=_cparams(("parallel", "parallel", "arbitrary")),
        name="fox_attention",
    )(fox, fox, fox, cs)


def _rel_bucket_np(dist):
    max_exact = REL_BUCKETS // 2
    d = np.maximum(dist.astype(np.float32), np.float32(1.0))
    ratio = np.log(d / np.float32(max_exact)).astype(np.float32) / np.float32(
        math.log(REL_MAX_DISTANCE / max_exact))
    large = max_exact + (ratio * np.float32(REL_BUCKETS - max_exact)).astype(np.int32)
    large = np.minimum(large, REL_BUCKETS - 1)
    return np.where(dist < max_exact, dist, large)


def _block_perm(dilation):
    n_chunks = MAX_DIL // dilation
    rows = DIL_BLOCK // n_chunks
    a = np.arange(n_chunks)[:, None]
    nn = np.arange(rows)[None, :]
    return (nn * n_chunks + a).reshape(-1)


def _bucket_tables():
    tabs = []
    for window, dilation in DIL_PATTERNS:
        span = window // dilation
        perm = _block_perm(dilation)
        i = perm[:, None]
        j = np.concatenate([perm, DIL_BLOCK + perm])[None, :]
        rel = DIL_BLOCK + i - j
        in_band = (rel >= 0) & (rel <= span)
        bucket = _rel_bucket_np(np.clip(rel, 0, span) * dilation)
        full = np.where(in_band, bucket, -1).astype(np.int32)
        first = np.where(j >= DIL_BLOCK, full, -1).astype(np.int32)
        tabs.append(np.stack([full, first]))
    return np.stack(tabs)


def _bias_table_kernel(rel_bias_ref, idx_ref, tab_ref):
    h = pl.program_id(1)
    idx = idx_ref[...]
    tab = jnp.full(idx.shape, NEG_INF, F32)
    for b in range(REL_BUCKETS):
        tab = jnp.where(idx == b, rel_bias_ref[b, h] * LOG2E, tab)
    tab_ref[...] = tab


def _bias_tables(rel_bias):
    idx = jnp.asarray(_bucket_tables())
    P = len(DIL_PATTERNS)
    blk = (2, DIL_BLOCK, 2 * DIL_BLOCK)
    return pl.pallas_call(
        _bias_table_kernel,
        grid=(P, N_HEADS_DIL),
        in_specs=[
            pl.BlockSpec(memory_space=pltpu.SMEM),
            pl.BlockSpec((None,) + blk, lambda p, h: (p, 0, 0, 0)),
        ],
        out_specs=pl.BlockSpec((None, None) + blk, lambda p, h: (p, h, 0, 0, 0)),
        out_shape=jax.ShapeDtypeStruct((P, N_HEADS_DIL) + blk, F32),
        compiler_params=_cparams(("parallel", "parallel")),
        name="dil_bias_tables",
    )(rel_bias, idx)


def _dil_kernel(q_ref, k_ref, v_ref, tab_ref, o_ref, op_ref, lse_ref, *, scale, group):
    n_sub = q_ref.shape[1]
    n_pat = len(DIL_PATTERNS)
    scale2 = scale * LOG2E
    ones = jnp.ones((2 * DIL_BLOCK, HEAD_DIM), BF16)

    def rows_at(start, rows):
        return pl.ds(start if isinstance(start, int) else pl.multiple_of(start, rows), rows)

    def gather(ref, dilation, r0, starts):
        n_chunks = MAX_DIL // dilation
        rows = DIL_BLOCK // n_chunks
        parts = [ref[r0 + a * dilation, rows_at(s, rows), :]
                 for s in starts for a in range(n_chunks)]
        return parts[0] if len(parts) == 1 else jnp.concatenate(parts, axis=0)

    def scatter(ref, val, dilation, r0, start):
        n_chunks = MAX_DIL // dilation
        rows = DIL_BLOCK // n_chunks
        for a in range(n_chunks):
            ref[r0 + a * dilation, rows_at(start, rows), :] = val[a * rows:(a + 1) * rows]

    def block(p_idx, dilation, r0, blk):
        rows = DIL_BLOCK * dilation // MAX_DIL
        cur = blk * rows
        q = gather(q_ref, dilation, r0, [cur]).astype(BF16)
        if isinstance(blk, int) and blk == 0:
            starts = [cur]
            tab = tab_ref[p_idx, 1, :, DIL_BLOCK:]
        elif isinstance(blk, int):
            starts = [cur - rows, cur]
            tab = tab_ref[p_idx, 0]
        else:
            starts = [jnp.maximum(blk - 1, 0) * rows, cur]
            tab = tab_ref[p_idx, (blk == 0).astype(jnp.int32)]
        k = gather(k_ref, dilation, r0, starts).astype(BF16)
        v = gather(v_ref, dilation, r0, starts).astype(BF16)
        s2 = lax.dot_general(q, k, (((1,), (1,)), ((), ())), preferred_element_type=F32)
        s2 = s2 * scale2 + tab
        m = jnp.max(s2, axis=-1, keepdims=True)
        p = jnp.exp2(s2 - m).astype(BF16)
        ol = jnp.dot(p, jnp.concatenate([v, ones[:v.shape[0]]], axis=1), preferred_element_type=F32)
        l = ol[:, HEAD_DIM:]
        scatter(op_ref.at[p_idx], ol[:, :HEAD_DIM] / l, dilation, r0, cur)
        scatter(lse_ref.at[p_idx], m + jnp.log2(l), dilation, r0, cur)

    for p_idx, (_, dilation) in enumerate(DIL_PATTERNS):
        n_blocks = (n_sub * MAX_DIL // dilation) // DIL_BLOCK
        if n_blocks <= 2:
            per_iter = group // n_blocks

            def body(g, carry, p_idx=p_idx, dilation=dilation, n_blocks=n_blocks, per_iter=per_iter):
                for i in range(per_iter):
                    for blk in range(n_blocks):
                        block(p_idx, dilation, g * per_iter + i, blk)
                return carry

            lax.fori_loop(0, dilation // per_iter, body, 0)
        else:
            per_iter = max(group // dilation, 1)

            def body(g, carry, p_idx=p_idx, dilation=dilation, per_iter=per_iter):
                for i in range(per_iter):
                    for r0 in range(dilation):
                        block(p_idx, dilation, r0, g * per_iter + i)
                return carry

            lax.fori_loop(0, n_blocks // per_iter, body, 0)

    chunk = DIL_BLOCK

    def combine(t, carry):
        r = t // (n_sub // chunk)
        rows = pl.ds(pl.multiple_of((t % (n_sub // chunk)) * chunk, chunk), chunk)
        lse = [lse_ref[p, r, rows, :] for p in range(n_pat)]
        top = functools.reduce(jnp.maximum, lse)
        w = [jnp.exp2(x - top) for x in lse]
        num = functools.reduce(lambda a, b: a + b, [w[p] * op_ref[p, r, rows, :] for p in range(n_pat)])
        o_ref[r, rows, :] = num / functools.reduce(lambda a, b: a + b, w)
        return carry

    lax.fori_loop(0, MAX_DIL * (n_sub // chunk), combine, 0)


def _dil_attention(qkv, tabs, B, S, *, group=8):
    H = N_HEADS_DIL
    n_sub = S // MAX_DIL
    P = len(DIL_PATTERNS)
    kern = functools.partial(_dil_kernel, scale=HEAD_DIM ** -0.5, group=group)
    blk = (None, MAX_DIL, n_sub, HEAD_DIM)
    return pl.pallas_call(
        kern,
        grid=(B, H),
        in_specs=[
            pl.BlockSpec(blk, lambda b, h: (b, 0, 0, h)),
            pl.BlockSpec(blk, lambda b, h: (b, 0, 0, H + h)),
            pl.BlockSpec(blk, lambda b, h: (b, 0, 0, 2 * H + h)),
            pl.BlockSpec((P, None, 2, DIL_BLOCK, 2 * DIL_BLOCK), lambda b, h: (0, h, 0, 0, 0)),
        ],
        out_specs=pl.BlockSpec(blk, lambda b, h: (b, 0, 0, h)),
        out_shape=jax.ShapeDtypeStruct((B, MAX_DIL, n_sub, D_DIL), F32),
        scratch_shapes=[
            pltpu.VMEM((P, MAX_DIL, n_sub, HEAD_DIM), F32),
            pltpu.VMEM((P, MAX_DIL, n_sub, HEAD_DIM), F32),
        ],
        compiler_params=_cparams(("parallel", "parallel")),
        name="dil_attention",
    )(qkv, qkv, qkv, tabs)


def _out_proj_kernel(ya_ref, yb_ref, ga_ref, gb_ref, perm_ref, w_ref, x_ref, o_ref):
    tm = ya_ref.shape[0]
    per = REGROUP // MAX_DIL
    for c in range(tm // REGROUP):
        rows = slice(c * REGROUP, (c + 1) * REGROUP)
        ya = ya_ref[rows, :]
        mix_a = (ya * _rms_scale(ya) * ga_ref[...]).astype(BF16)
        yb = jnp.concatenate([yb_ref[r, c * per:(c + 1) * per, :] for r in range(MAX_DIL)], axis=0)
        mix_b = (yb * _rms_scale(yb) * gb_ref[...]).astype(BF16)
        mix_b = jnp.dot(perm_ref[...], mix_b, preferred_element_type=F32).astype(BF16)
        mix = jnp.concatenate([mix_a, mix_b], axis=1)
        o_ref[rows, :] = x_ref[rows, :] + jnp.dot(mix, w_ref[...], preferred_element_type=F32)


def _out_proj(ya, yb, ga, gb, perm, w, x, layer, B, S, *, tm=512):
    T, D = x.shape
    per_b = S // tm
    return pl.pallas_call(
        _out_proj_kernel,
        grid=(T // tm,),
        in_specs=[
            pl.BlockSpec((tm, D_FOX), lambda i: (i, 0)),
            pl.BlockSpec((None, MAX_DIL, tm // MAX_DIL, D_DIL), lambda i: (i // per_b, 0, i % per_b, 0)),
            pl.BlockSpec((None, 1, D_FOX), lambda i: (layer, 0, 0)),
            pl.BlockSpec((None, 1, D_DIL), lambda i: (layer, 0, 0)),
            pl.BlockSpec((REGROUP, REGROUP), lambda i: (0, 0)),
            pl.BlockSpec((None, D_FOX + D_DIL, D), lambda i: (layer, 0, 0)),
            pl.BlockSpec((tm, D), lambda i: (i, 0)),
        ],
        out_specs=pl.BlockSpec((tm, D), lambda i: (i, 0)),
        out_shape=jax.ShapeDtypeStruct((T, D), F32),
        compiler_params=_cparams(("parallel",)),
        name="out_proj",
    )(ya, yb, ga, gb, perm, w, x)


def _mlp_kernel(x_ref, g_ref, w1_ref, w2_ref, fg_ref, o_ref, xn_ref, acc_ref, *, final_norm, sub):
    f = pl.program_id(1)
    tf = w1_ref.shape[1]

    @pl.when(f == 0)
    def _():
        x = x_ref[...]
        xn_ref[...] = (x * _rms_scale(x) * g_ref[...]).astype(BF16)
        acc_ref[...] = x

    upd = None
    for c in range(tf // sub):
        h = jnp.dot(xn_ref[...], w1_ref[:, c * sub:(c + 1) * sub], preferred_element_type=F32)
        h = jnp.square(jnp.maximum(h, 0.0)).astype(BF16)
        d = jnp.dot(h, w2_ref[c * sub:(c + 1) * sub, :], preferred_element_type=F32)
        upd = d if upd is None else upd + d
    acc_ref[...] += upd

    @pl.when(f == pl.num_programs(1) - 1)
    def _():
        y = acc_ref[...]
        if final_norm:
            y = y * _rms_scale(y) * fg_ref[...]
        o_ref[...] = y


def _mlp(x, g, w1, w2, fg, layer, *, final_norm, tm=512, tf=1024, sub=512):
    T, D = x.shape
    F = w1.shape[2]
    kern = functools.partial(_mlp_kernel, final_norm=final_norm, sub=sub)
    return pl.pallas_call(
        kern,
        grid=(T // tm, F // tf),
        in_specs=[
            pl.BlockSpec((tm, D), lambda i, f: (i, 0)),
            pl.BlockSpec((None, 1, D), lambda i, f: (layer, 0, 0)),
            pl.BlockSpec((None, D, tf), lambda i, f: (layer, 0, f)),
            pl.BlockSpec((None, tf, D), lambda i, f: (layer, f, 0)),
            pl.BlockSpec((1, D), lambda i, f: (0, 0)),
        ],
        out_specs=pl.BlockSpec((tm, D), lambda i, f: (i, 0)),
        out_shape=jax.ShapeDtypeStruct((T, D), F32),
        scratch_shapes=[pltpu.VMEM((tm, D), BF16), pltpu.VMEM((tm, D), F32)],
        compiler_params=_cparams(("parallel", "arbitrary")),
        name="mlp",
    )(x, g, w1, w2, fg)


def kernel(x, norm1_g, w_in, forget_b, rel_bias, outnorm_a_g, outnorm_b_g, w_out,
           norm2_g, w_mlp_in, w_mlp_out, final_norm_g):
    B, S, D = x.shape
    T = B * S
    depth = w_in.shape[0]
    f0 = 3 * D_FOX
    assert S % (MAX_DIL * DIL_BLOCK) == 0 and w_in.shape[2] == f0 + N_HEADS_FOX + 3 * D_DIL

    w_qkv = jnp.concatenate([w_in[:, :, :f0], w_in[:, :, f0 + N_HEADS_FOX:]], axis=2).astype(BF16)
    w_f = jnp.pad(w_in[:, :, f0:f0 + N_HEADS_FOX], ((0, 0), (0, 0), (0, LANES - N_HEADS_FOX))).astype(BF16)
    fb = jnp.pad(forget_b.astype(F32), ((0, 0), (0, LANES - N_HEADS_FOX))).reshape(depth, 1, LANES)
    w_o = w_out.astype(BF16)
    w1 = w_mlp_in.astype(BF16)
    w2 = w_mlp_out.astype(BF16)
    g1 = norm1_g.astype(F32).reshape(depth, 1, D)
    g2 = norm2_g.astype(F32).reshape(depth, 1, D)
    ga = outnorm_a_g.astype(F32).reshape(depth, 1, D_FOX)
    gb = outnorm_b_g.astype(F32).reshape(depth, 1, D_DIL)
    fg = final_norm_g.astype(F32).reshape(1, D)
    perm = _regroup_matrix()
    tabs = _bias_tables(rel_bias.astype(F32))

    xf = x.reshape(T, D)
    for l in range(depth):
        fox, dil, logf = _in_proj(xf, g1, w_qkv, w_f, fb, perm, l, B, S)
        cs = _fox_cumsum(logf, B, S)
        ya = _fox_attention(fox.reshape(B, S, 3 * D_FOX), cs.reshape(B, S, LANES), B, S)
        yb = _dil_attention(dil, tabs, B, S)
        xf = _out_proj(ya.reshape(T, D_FOX), yb, ga, gb, perm, w_o, xf, l, B, S)
        xf = _mlp(xf, g2, w1, w2, fg, l, final_norm=(l == depth - 1))
    return xf.reshape(B, S, D)
```

```python
import functools
import math

import numpy as np
import jax
import jax.numpy as jnp
from jax import lax
from jax.experimental import pallas as pl
from jax.experimental.pallas import tpu as pltpu

F32 = jnp.float32
BF16 = jnp.bfloat16

HEAD_DIM = 128
N_HEADS_FOX = 8
N_HEADS_DIL = 8
D_FOX = N_HEADS_FOX * HEAD_DIM
D_DIL = N_HEADS_DIL * HEAD_DIM
DIL_PATTERNS = ((128, 1), (512, 4), (2048, 16))
DIL_BLOCK = 128
MAX_DIL = 16
REGROUP = MAX_DIL * MAX_DIL
REL_BUCKETS = 32
REL_MAX_DISTANCE = 2048
NORM_EPS = 1e-6
NEG_INF = -1e30
LOG2E = math.log2(math.e)

LANES = 128
IN_PROJ_TN = 512
MLP_TF = 1024
VMEM_LIMIT = 56 * 1024 * 1024


def _cparams(sem):
    return pltpu.CompilerParams(dimension_semantics=sem, vmem_limit_bytes=VMEM_LIMIT)


def _rms_scale(x):
    return lax.rsqrt(jnp.mean(x * x, axis=-1, keepdims=True) + NORM_EPS)


def _regroup_matrix():
    t = np.arange(REGROUP)
    src = (t % MAX_DIL) * MAX_DIL + t // MAX_DIL
    return jnp.asarray(np.eye(REGROUP, dtype=np.float32)[src], dtype=BF16)


def _in_proj_kernel(x_ref, g_ref, w_ref, wf_ref, fb_ref, perm_ref, fox_ref, dil_ref, logf_ref,
                    xn_ref, xnp_ref, *, n_fox_tiles):
    j = pl.program_id(1)
    tm = x_ref.shape[0]

    @pl.when(j == 0)
    def _():
        for ci in range(tm // REGROUP):
            rows = slice(ci * REGROUP, (ci + 1) * REGROUP)
            x = x_ref[rows, :]
            xn = (x * _rms_scale(x) * g_ref[...]).astype(BF16)
            xn_ref[rows, :] = xn
            xnp_ref[rows, :] = jnp.dot(perm_ref[...], xn, preferred_element_type=F32).astype(BF16)
            f = jnp.dot(xn, wf_ref[...], preferred_element_type=F32) + fb_ref[...]
            logf_ref[rows, :] = jnp.minimum(f, 0.0) - jnp.log1p(jnp.exp(-jnp.abs(f)))

    @pl.when(j < n_fox_tiles)
    def _():
        fox_ref[...] = jnp.dot(xn_ref[...], w_ref[...], preferred_element_type=F32).astype(BF16)

    @pl.when(j >= n_fox_tiles)
    def _():
        acc = jnp.dot(xnp_ref[...], w_ref[...], preferred_element_type=F32)
        per = REGROUP // MAX_DIL
        for c in range(tm // REGROUP):
            for r in range(MAX_DIL):
                lo = c * REGROUP + r * per
                dil_ref[r, c * per:(c + 1) * per, :] = acc[lo:lo + per, :]


def _in_proj(x, g, w_qkv, w_f, fb, perm, layer, B, S, *, tm=1024):
    T, D = x.shape
    tn = w_qkv.shape[3]
    n_fox = 3 * D_FOX
    n_dil = 3 * D_DIL
    nf, nd = n_fox // tn, n_dil // tn
    per_b = S // tm
    kern = functools.partial(_in_proj_kernel, n_fox_tiles=nf)
    return pl.pallas_call(
        kern,
        grid=(T // tm, nf + nd),
        in_specs=[
            pl.BlockSpec((tm, D), lambda i, j: (i, 0)),
            pl.BlockSpec((None, 1, D), lambda i, j: (layer, 0, 0)),
            pl.BlockSpec((None, None, D, tn), lambda i, j: (layer, j, 0, 0)),
            pl.BlockSpec((None, D, LANES), lambda i, j: (layer, 0, 0)),
            pl.BlockSpec((None, 1, LANES), lambda i, j: (layer, 0, 0)),
            pl.BlockSpec((REGROUP, REGROUP), lambda i, j: (0, 0)),
        ],
        out_specs=[
            pl.BlockSpec((tm, tn), lambda i, j: (i, jnp.minimum(j, nf - 1))),
            pl.BlockSpec((None, MAX_DIL, tm // MAX_DIL, tn),
                         lambda i, j: (i // per_b, 0, i % per_b, jnp.maximum(j - nf, 0))),
            pl.BlockSpec((tm, LANES), lambda i, j: (i, 0)),
        ],
        out_shape=[
            jax.ShapeDtypeStruct((T, n_fox), BF16),
            jax.ShapeDtypeStruct((B, MAX_DIL, S // MAX_DIL, n_dil), F32),
            jax.ShapeDtypeStruct((T, LANES), F32),
        ],
        scratch_shapes=[pltpu.VMEM((tm, D), BF16), pltpu.VMEM((tm, D), BF16)],
        compiler_params=_cparams(("parallel", "arbitrary")),
        name="in_proj",
    )(x, g, w_qkv, w_f, fb, perm)


def _split3(x):
    hi = x.astype(BF16)
    r1 = x - hi.astype(F32)
    mid = r1.astype(BF16)
    lo = (r1 - mid.astype(F32)).astype(BF16)
    return hi, mid, lo


def _cumsum_kernel(logf_ref, cs_ref, *, chunk):
    S = logf_ref.shape[0]
    n_chunks = S // chunk
    row = lax.broadcasted_iota(jnp.int32, (chunk, chunk), 0)
    col = lax.broadcasted_iota(jnp.int32, (chunk, chunk), 1)
    tri = (col <= row).astype(BF16)
    carry = jnp.zeros((1, LANES), F32)
    for ci in range(n_chunks):
        seg = logf_ref[ci * chunk:(ci + 1) * chunk, :]
        hi, mid, lo = _split3(seg)
        within = (jnp.dot(tri, hi, preferred_element_type=F32)
                  + jnp.dot(tri, mid, preferred_element_type=F32)
                  + jnp.dot(tri, lo, preferred_element_type=F32))
        within = within + carry
        cs_ref[ci * chunk:(ci + 1) * chunk, :] = within
        carry = within[chunk - 1:chunk, :]


def _fox_cumsum(logf, B, S):
    kern = functools.partial(_cumsum_kernel, chunk=128)
    return pl.pallas_call(
        kern,
        grid=(B,),
        in_specs=[pl.BlockSpec((S, LANES), lambda b: (b, 0))],
        out_specs=pl.BlockSpec((S, LANES), lambda b: (b, 0)),
        out_shape=jax.ShapeDtypeStruct((B * S, LANES), F32),
        compiler_params=_cparams(("parallel",)),
        name="fox_cumsum",
    )(logf)


def _fox_kernel(q_ref, k_ref, v_ref, cs_ref, o_ref, kaug_ref, vaug_ref, qaug_ref, s0_ref, s1_ref,
                p_ref, alpha_ref, m_ref, accl_ref, *, tq, scale, chunk):
    h = pl.program_id(1)
    qi = pl.program_id(2)
    S = k_ref.shape[0]
    tk = tq // 2
    scale2 = scale * LOG2E
    nt = (((1,), (1,)), ((), ()))

    @pl.when(qi == 0)
    def _():
        lane = lax.broadcasted_iota(jnp.int32, (tk, LANES), 1)

        def setup(ci, carry):
            rows = pl.ds(pl.multiple_of(ci * tk, tk), tk)
            cs = cs_ref[rows, :]
            col = jnp.sum(jnp.where(lane == h, cs, 0.0), axis=-1, keepdims=True)
            hi, mid, lo = _split3(col * (-1.0 / scale))
            aug = jnp.where(lane == 0, hi.astype(F32),
                            jnp.where(lane == 1, mid.astype(F32),
                                      jnp.where(lane == 2, lo.astype(F32), 0.0)))
            kaug_ref[rows, :HEAD_DIM] = k_ref[rows, :]
            kaug_ref[rows, HEAD_DIM:] = aug.astype(BF16)
            vaug_ref[rows, :HEAD_DIM] = v_ref[rows, :]
            vaug_ref[rows, HEAD_DIM:] = jnp.ones((tk, LANES), BF16)
            return carry

        lax.fori_loop(0, S // tk, setup, 0)

    lane = lax.broadcasted_iota(jnp.int32, (tq, LANES), 1)
    qaug_ref[:, :HEAD_DIM] = q_ref[...]
    qaug_ref[:, HEAD_DIM:] = jnp.where(lane < 3, 1.0, 0.0).astype(BF16)
    accl_ref[...] = jnp.zeros(accl_ref.shape, F32)
    m_ref[...] = jnp.full(m_ref.shape, NEG_INF, F32)
    p_ref[...] = jnp.zeros(p_ref.shape, BF16)
    alpha_ref[...] = jnp.ones(alpha_ref.shape, F32)

    def key_rows(blk):
        return pl.ds(pl.multiple_of(blk * tk, tk), tk)

    def scores(blk, s_ref, row_lo=0):
        s_ref[row_lo:, :] = lax.dot_general(qaug_ref[row_lo:, :], kaug_ref[key_rows(blk), :], nt,
                                            preferred_element_type=F32)

    def pv_update(blk, row_lo=0):
        oa = jnp.dot(p_ref[row_lo:, :], vaug_ref[key_rows(blk), :], preferred_element_type=F32)
        a = alpha_ref[row_lo:, :]
        accl_ref[row_lo:, :HEAD_DIM] = a * accl_ref[row_lo:, :HEAD_DIM] + oa[:, :HEAD_DIM]
        accl_ref[row_lo:, HEAD_DIM:] = a * accl_ref[row_lo:, HEAD_DIM:] + oa[:, HEAD_DIM:]

    def softmax(s_ref, row_lo, row_hi, masked):
        for r0 in range(row_lo, row_hi, chunk):
            rows = slice(r0, r0 + chunk)
            tiles = [s_ref[rows, j * LANES:(j + 1) * LANES] for j in range(tk // LANES)]
            if masked:
                row = lax.broadcasted_iota(jnp.int32, (chunk, LANES), 0) + (r0 - row_lo)
                col = lax.broadcasted_iota(jnp.int32, (chunk, LANES), 1)
                tiles = [jnp.where(col + j * LANES <= row, t, NEG_INF) for j, t in enumerate(tiles)]
            top = functools.reduce(jnp.maximum, tiles)
            m_prev = m_ref[rows, :]
            m_new = jnp.maximum(m_prev, jnp.max(top, axis=-1, keepdims=True) * scale2)
            alpha_ref[rows, :] = jnp.exp2(m_prev - m_new)
            m_ref[rows, :] = m_new
            for j, t in enumerate(tiles):
                p_ref[rows, j * LANES:(j + 1) * LANES] = jnp.exp2(t * scale2 - m_new).astype(BF16)

    def pair(t, carry):
        j = 2 * t
        scores(j + 1, s1_ref)
        pv_update(jnp.maximum(j - 1, 0))
        softmax(s0_ref, 0, tq, False)
        scores(j + 2, s0_ref)
        pv_update(j)
        softmax(s1_ref, 0, tq, False)
        return carry

    scores(0, s0_ref)
    lax.fori_loop(0, qi, pair, 0)
    d0 = 2 * qi
    scores(d0 + 1, s1_ref, row_lo=tk)
    pv_update(jnp.maximum(d0 - 1, 0))
    softmax(s0_ref, 0, tk, True)
    softmax(s0_ref, tk, tq, False)
    pv_update(d0)
    softmax(s1_ref, tk, tq, True)
    pv_update(d0 + 1, row_lo=tk)
    o_ref[...] = accl_ref[:, :HEAD_DIM] / accl_ref[:, HEAD_DIM:]


def _fox_attention(fox, cs, B, S, *, tq=1024, chunk=64):
    H = N_HEADS_FOX
    tk = tq // 2
    kern = functools.partial(_fox_kernel, tq=tq, scale=HEAD_DIM ** -0.5, chunk=chunk)
    return pl.pallas_call(
        kern,
        grid=(B, H, S // tq),
        in_specs=[
            pl.BlockSpec((None, tq, HEAD_DIM), lambda b, h, i: (b, i, h)),
            pl.BlockSpec((None, S, HEAD_DIM), lambda b, h, i: (b, 0, H + h)),
            pl.BlockSpec((None, S, HEAD_DIM), lambda b, h, i: (b, 0, 2 * H + h)),
            pl.BlockSpec((None, S, LANES), lambda b, h, i: (b, 0, 0)),
        ],
        out_specs=pl.BlockSpec((None, tq, HEAD_DIM), lambda b, h, i: (b, i, h)),
        out_shape=jax.ShapeDtypeStruct((B, S, D_FOX), F32),
        scratch_shapes=[
            pltpu.VMEM((S, 2 * HEAD_DIM), BF16),
            pltpu.VMEM((S, 2 * HEAD_DIM), BF16),
            pltpu.VMEM((tq, 2 * HEAD_DIM), BF16),
            pltpu.VMEM((tq, tk), F32),
            pltpu.VMEM((tq, tk), F32),
            pltpu.VMEM((tq, tk), BF16),
            pltpu.VMEM((tq, LANES), F32),
            pltpu.VMEM((tq, LANES), F32),
            pltpu.VMEM((tq, 2 * HEAD_DIM), F32),
        ],
        compiler_params=_cparams(("parallel", "parallel", "arbitrary")),
        name="fox_attention",
    )(fox, fox, fox, cs)


def _rel_bucket_np(dist):
    max_exact = REL_BUCKETS // 2
    d = np.maximum(dist.astype(np.float32), np.float32(1.0))
    ratio = np.log(d / np.float32(max_exact)).astype(np.float32) / np.float32(
        math.log(REL_MAX_DISTANCE / max_exact))
    large = max_exact + (ratio * np.float32(REL_BUCKETS - max_exact)).astype(np.int32)
    large = np.minimum(large, REL_BUCKETS - 1)
    return np.where(dist < max_exact, dist, large)


def _block_perm(dilation):
    n_chunks = MAX_DIL // dilation
    rows = DIL_BLOCK // n_chunks
    a = np.arange(n_chunks)[:, None]
    nn = np.arange(rows)[None, :]
    return (nn * n_chunks + a).reshape(-1)


def _bucket_tables():
    tabs = []
    for window, dilation in DIL_PATTERNS:
        span = window // dilation
        perm = _block_perm(dilation)
        i = perm[:, None]
        j = np.concatenate([perm, DIL_BLOCK + perm])[None, :]
        rel = DIL_BLOCK + i - j
        in_band = (rel >= 0) & (rel <= span)
        bucket = _rel_bucket_np(np.clip(rel, 0, span) * dilation)
        full = np.where(in_band, bucket, -1).astype(np.int32)
        first = np.where(j >= DIL_BLOCK, full, -1).astype(np.int32)
        tabs.append(np.stack([full, first]))
    return np.stack(tabs)


def _bias_table_kernel(rel_bias_ref, idx_ref, tab_ref):
    h = pl.program_id(1)
    idx = idx_ref[...]
    tab = jnp.full(idx.shape, NEG_INF, F32)
    for b in range(REL_BUCKETS):
        tab = jnp.where(idx == b, rel_bias_ref[b, h] * LOG2E, tab)
    tab_ref[...] = tab


def _bias_tables(rel_bias):
    idx = jnp.asarray(_bucket_tables())
    P = len(DIL_PATTERNS)
    blk = (2, DIL_BLOCK, 2 * DIL_BLOCK)
    return pl.pallas_call(
        _bias_table_kernel,
        grid=(P, N_HEADS_DIL),
        in_specs=[
            pl.BlockSpec(memory_space=pltpu.SMEM),
            pl.BlockSpec((None,) + blk, lambda p, h: (p, 0, 0, 0)),
        ],
        out_specs=pl.BlockSpec((None, None) + blk, lambda p, h: (p, h, 0, 0, 0)),
        out_shape=jax.ShapeDtypeStruct((P, N_HEADS_DIL) + blk, F32),
        compiler_params=_cparams(("parallel", "parallel")),
        name="dil_bias_tables",
    )(rel_bias, idx)


def _dil_kernel(q_ref, k_ref, v_ref, tab_ref, o_ref, op_ref, lse_ref, *, scale, group):
    n_sub = q_ref.shape[1]
    n_pat = len(DIL_PATTERNS)
    scale2 = scale * LOG2E
    ones = jnp.ones((2 * DIL_BLOCK, HEAD_DIM), BF16)

    def rows_at(start, rows):
        return pl.ds(start if isinstance(start, int) else pl.multiple_of(start, rows), rows)

    def gather(ref, dilation, r0, starts):
        n_chunks = MAX_DIL // dilation
        rows = DIL_BLOCK // n_chunks
        parts = [ref[r0 + a * dilation, rows_at(s, rows), :]
                 for s in starts for a in range(n_chunks)]
        return parts[0] if len(parts) == 1 else jnp.concatenate(parts, axis=0)

    def scatter(ref, val, dilation, r0, start):
        n_chunks = MAX_DIL // dilation
        rows = DIL_BLOCK // n_chunks
        for a in range(n_chunks):
            ref[r0 + a * dilation, rows_at(start, rows), :] = val[a * rows:(a + 1) * rows]

    def block(p_idx, dilation, r0, blk):
        rows = DIL_BLOCK * dilation // MAX_DIL
        cur = blk * rows
        q = gather(q_ref, dilation, r0, [cur]).astype(BF16)
        if isinstance(blk, int) and blk == 0:
            starts = [cur]
            tab = tab_ref[p_idx, 1, :, DIL_BLOCK:]
        elif isinstance(blk, int):
            starts = [cur - rows, cur]
            tab = tab_ref[p_idx, 0]
        else:
            starts = [jnp.maximum(blk - 1, 0) * rows, cur]
            tab = tab_ref[p_idx, (blk == 0).astype(jnp.int32)]
        k = gather(k_ref, dilation, r0, starts).astype(BF16)
        v = gather(v_ref, dilation, r0, starts).astype(BF16)
        s2 = lax.dot_general(q, k, (((1,), (1,)), ((), ())), preferred_element_type=F32)
        s2 = s2 * scale2 + tab
        m = jnp.max(s2, axis=-1, keepdims=True)
        p = jnp.exp2(s2 - m).astype(BF16)
        ol = jnp.dot(p, jnp.concatenate([v, ones[:v.shape[0]]], axis=1), preferred_element_type=F32)
        l = ol[:, HEAD_DIM:]
        scatter(op_ref.at[p_idx], ol[:, :HEAD_DIM] / l, dilation, r0, cur)
        scatter(lse_ref.at[p_idx], m + jnp.log2(l), dilation, r0, cur)

    for p_idx, (_, dilation) in enumerate(DIL_PATTERNS):
        n_blocks = (n_sub * MAX_DIL // dilation) // DIL_BLOCK
        if n_blocks <= 2:
            per_iter = group // n_blocks

            def body(g, carry, p_idx=p_idx, dilation=dilation, n_blocks=n_blocks, per_iter=per_iter):
                for i in range(per_iter):
                    for blk in range(n_blocks):
                        block(p_idx, dilation, g * per_iter + i, blk)
                return carry

            lax.fori_loop(0, dilation // per_iter, body, 0)
        else:
            per_iter = max(group // dilation, 1)

            def body(g, carry, p_idx=p_idx, dilation=dilation, per_iter=per_iter):
                for i in range(per_iter):
                    for r0 in range(dilation):
                        block(p_idx, dilation, r0, g * per_iter + i)
                return carry

            lax.fori_loop(0, n_blocks // per_iter, body, 0)

    chunk = DIL_BLOCK

    def combine(t, carry):
        r = t // (n_sub // chunk)
        rows = pl.ds(pl.multiple_of((t % (n_sub // chunk)) * chunk, chunk), chunk)
        lse = [lse_ref[p, r, rows, :] for p in range(n_pat)]
        top = functools.reduce(jnp.maximum, lse)
        w = [jnp.exp2(x - top) for x in lse]
        num = functools.reduce(lambda a, b: a + b, [w[p] * op_ref[p, r, rows, :] for p in range(n_pat)])
        o_ref[r, rows, :] = num / functools.reduce(lambda a, b: a + b, w)
        return carry

    lax.fori_loop(0, MAX_DIL * (n_sub // chunk), combine, 0)


def _dil_attention(qkv, tabs, B, S, *, group=8):
    H = N_HEADS_DIL
    n_sub = S // MAX_DIL
    P = len(DIL_PATTERNS)
    kern = functools.partial(_dil_kernel, scale=HEAD_DIM ** -0.5, group=group)
    blk = (None, MAX_DIL, n_sub, HEAD_DIM)
    return pl.pallas_call(
        kern,
        grid=(B, H),
        in_specs=[
            pl.BlockSpec(blk, lambda b, h: (b, 0, 0, h)),
            pl.BlockSpec(blk, lambda b, h: (b, 0, 0, H + h)),
            pl.BlockSpec(blk, lambda b, h: (b, 0, 0, 2 * H + h)),
            pl.BlockSpec((P, None, 2, DIL_BLOCK, 2 * DIL_BLOCK), lambda b, h: (0, h, 0, 0, 0)),
        ],
        out_specs=pl.BlockSpec(blk, lambda b, h: (b, 0, 0, h)),
        out_shape=jax.ShapeDtypeStruct((B, MAX_DIL, n_sub, D_DIL), F32),
        scratch_shapes=[
            pltpu.VMEM((P, MAX_DIL, n_sub, HEAD_DIM), F32),
            pltpu.VMEM((P, MAX_DIL, n_sub, HEAD_DIM), F32),
        ],
        compiler_params=_cparams(("parallel", "parallel")),
        name="dil_attention",
    )(qkv, qkv, qkv, tabs)


def _out_proj_kernel(ya_ref, yb_ref, ga_ref, gb_ref, perm_ref, w_ref, x_ref, o_ref):
    tm = ya_ref.shape[0]
    per = REGROUP // MAX_DIL
    for c in range(tm // REGROUP):
        rows = slice(c * REGROUP, (c + 1) * REGROUP)
        ya = ya_ref[rows, :]
        mix_a = (ya * _rms_scale(ya) * ga_ref[...]).astype(BF16)
        yb = jnp.concatenate([yb_ref[r, c * per:(c + 1) * per, :] for r in range(MAX_DIL)], axis=0)
        mix_b = (yb * _rms_scale(yb) * gb_ref[...]).astype(BF16)
        mix_b = jnp.dot(perm_ref[...], mix_b, preferred_element_type=F32).astype(BF16)
        mix = jnp.concatenate([mix_a, mix_b], axis=1)
        o_ref[rows, :] = x_ref[rows, :] + jnp.dot(mix, w_ref[...], preferred_element_type=F32)


def _out_proj(ya, yb, ga, gb, perm, w, x, layer, B, S, *, tm=512):
    T, D = x.shape
    per_b = S // tm
    return pl.pallas_call(
        _out_proj_kernel,
        grid=(T // tm,),
        in_specs=[
            pl.BlockSpec((tm, D_FOX), lambda i: (i, 0)),
            pl.BlockSpec((None, MAX_DIL, tm // MAX_DIL, D_DIL), lambda i: (i // per_b, 0, i % per_b, 0)),
            pl.BlockSpec((None, 1, D_FOX), lambda i: (layer, 0, 0)),
            pl.BlockSpec((None, 1, D_DIL), lambda i: (layer, 0, 0)),
            pl.BlockSpec((REGROUP, REGROUP), lambda i: (0, 0)),
            pl.BlockSpec((None, D_FOX + D_DIL, D), lambda i: (layer, 0, 0)),
            pl.BlockSpec((tm, D), lambda i: (i, 0)),
        ],
        out_specs=pl.BlockSpec((tm, D), lambda i: (i, 0)),
        out_shape=jax.ShapeDtypeStruct((T, D), F32),
        compiler_params=_cparams(("parallel",)),
        name="out_proj",
    )(ya, yb, ga, gb, perm, w, x)


def _mlp_kernel(x_ref, g_ref, w1_ref, w2_ref, fg_ref, o_ref, xn_ref, acc_ref, *, final_norm, sub):
    f = pl.program_id(1)
    tf = w1_ref.shape[1]

    @pl.when(f == 0)
    def _():
        x = x_ref[...]
        xn_ref[...] = (x * _rms_scale(x) * g_ref[...]).astype(BF16)
        acc_ref[...] = x

    upd = None
    for c in range(tf // sub):
        h = jnp.dot(xn_ref[...], w1_ref[:, c * sub:(c + 1) * sub], preferred_element_type=F32)
        h = jnp.square(jnp.maximum(h, 0.0)).astype(BF16)
        d = jnp.dot(h, w2_ref[c * sub:(c + 1) * sub, :], preferred_element_type=F32)
        upd = d if upd is None else upd + d
    acc_ref[...] += upd

    @pl.when(f == pl.num_programs(1) - 1)
    def _():
        y = acc_ref[...]
        if final_norm:
            y = y * _rms_scale(y) * fg_ref[...]
        o_ref[...] = y


def _mlp(x, g, w1, w2, fg, layer, *, final_norm, tm=512, sub=512):
    T, D = x.shape
    tf = w1.shape[3]
    F = w2.shape[1]
    kern = functools.partial(_mlp_kernel, final_norm=final_norm, sub=sub)
    return pl.pallas_call(
        kern,
        grid=(T // tm, F // tf),
        in_specs=[
            pl.BlockSpec((tm, D), lambda i, f: (i, 0)),
            pl.BlockSpec((None, 1, D), lambda i, f: (layer, 0, 0)),
            pl.BlockSpec((None, None, D, tf), lambda i, f: (layer, f, 0, 0)),
            pl.BlockSpec((None, tf, D), lambda i, f: (layer, f, 0)),
            pl.BlockSpec((1, D), lambda i, f: (0, 0)),
        ],
        out_specs=pl.BlockSpec((tm, D), lambda i, f: (i, 0)),
        out_shape=jax.ShapeDtypeStruct((T, D), F32),
        scratch_shapes=[pltpu.VMEM((tm, D), BF16), pltpu.VMEM((tm, D), F32)],
        compiler_params=_cparams(("parallel", "arbitrary")),
        name="mlp",
    )(x, g, w1, w2, fg)


def kernel(x, norm1_g, w_in, forget_b, rel_bias, outnorm_a_g, outnorm_b_g, w_out,
           norm2_g, w_mlp_in, w_mlp_out, final_norm_g):
    B, S, D = x.shape
    T = B * S
    depth = w_in.shape[0]
    f0 = 3 * D_FOX
    assert S % (MAX_DIL * DIL_BLOCK) == 0 and w_in.shape[2] == f0 + N_HEADS_FOX + 3 * D_DIL

    def col_tiles(w, tn):
        return w.astype(BF16).reshape(depth, D, w.shape[2] // tn, tn).transpose(0, 2, 1, 3)

    w_qkv = jnp.concatenate([col_tiles(w_in[:, :, :f0], IN_PROJ_TN),
                             col_tiles(w_in[:, :, f0 + N_HEADS_FOX:], IN_PROJ_TN)], axis=1)
    w_f = jnp.pad(w_in[:, :, f0:f0 + N_HEADS_FOX], ((0, 0), (0, 0), (0, LANES - N_HEADS_FOX))).astype(BF16)
    fb = jnp.pad(forget_b.astype(F32), ((0, 0), (0, LANES - N_HEADS_FOX))).reshape(depth, 1, LANES)
    w_o = w_out.astype(BF16)
    w1 = col_tiles(w_mlp_in, MLP_TF)
    w2 = w_mlp_out.astype(BF16)
    g1 = norm1_g.astype(F32).reshape(depth, 1, D)
    g2 = norm2_g.astype(F32).reshape(depth, 1, D)
    ga = outnorm_a_g.astype(F32).reshape(depth, 1, D_FOX)
    gb = outnorm_b_g.astype(F32).reshape(depth, 1, D_DIL)
    fg = final_norm_g.astype(F32).reshape(1, D)
    perm = _regroup_matrix()
    tabs = _bias_tables(rel_bias.astype(F32))

    xf = x.reshape(T, D)
    for l in range(depth):
        fox, dil, logf = _in_proj(xf, g1, w_qkv, w_f, fb, perm, l, B, S)
        cs = _fox_cumsum(logf, B, S)
        ya = _fox_attention(fox.reshape(B, S, 3 * D_FOX), cs.reshape(B, S, LANES), B, S)
        yb = _dil_attention(dil, tabs, B, S)
        xf = _out_proj(ya.reshape(T, D_FOX), yb, ga, gb, perm, w_o, xf, l, B, S)
        xf = _mlp(xf, g2, w1, w2, fg, l, final_norm=(l == depth - 1))
    return xf.reshape(B, S, D)
```

```python
import functools
import math

import numpy as np
import jax
import jax.numpy as jnp
from jax import lax
from jax.experimental import pallas as pl
from jax.experimental.pallas import tpu as pltpu

F32 = jnp.float32
BF16 = jnp.bfloat16

HEAD_DIM = 128
N_HEADS_FOX = 8
N_HEADS_DIL = 8
D_FOX = N_HEADS_FOX * HEAD_DIM
D_DIL = N_HEADS_DIL * HEAD_DIM
DIL_PATTERNS = ((128, 1), (512, 4), (2048, 16))
DIL_BLOCK = 128
MAX_DIL = 16
REGROUP = MAX_DIL * MAX_DIL
REL_BUCKETS = 32
REL_MAX_DISTANCE = 2048
NORM_EPS = 1e-6
NEG_INF = -1e30
LOG2E = math.log2(math.e)

LANES = 128
IN_PROJ_TN = 512
MLP_TF = 1024
VMEM_LIMIT = 56 * 1024 * 1024


def _cparams(sem, flags=None):
    return pltpu.CompilerParams(dimension_semantics=sem, vmem_limit_bytes=VMEM_LIMIT, flags=flags)


def _rms_scale(x):
    return lax.rsqrt(jnp.mean(x * x, axis=-1, keepdims=True) + NORM_EPS)


def _regroup_matrix():
    t = np.arange(REGROUP)
    src = (t % MAX_DIL) * MAX_DIL + t // MAX_DIL
    return jnp.asarray(np.eye(REGROUP, dtype=np.float32)[src], dtype=BF16)


def _in_proj_kernel(x_ref, g_ref, wfox_ref, wdil_ref, wf_ref, fb_ref, perm_ref, fox_ref, dil_ref,
                    logf_ref, *, sub):
    tm = x_ref.shape[0]
    per = REGROUP // MAX_DIL
    for ci in range(tm // REGROUP):
        rows = slice(ci * REGROUP, (ci + 1) * REGROUP)
        x = x_ref[rows, :]
        xn = (x * _rms_scale(x) * g_ref[...]).astype(BF16)
        xnp = jnp.dot(perm_ref[...], xn, preferred_element_type=F32).astype(BF16)
        f = jnp.dot(xn, wf_ref[...], preferred_element_type=F32) + fb_ref[...]
        logf_ref[rows, :] = jnp.minimum(f, 0.0) - jnp.log1p(jnp.exp(-jnp.abs(f)))
        for s in range(0, wfox_ref.shape[1], sub):
            fox_ref[rows, s:s + sub] = jnp.dot(xn, wfox_ref[:, s:s + sub],
                                               preferred_element_type=F32).astype(BF16)
        for s in range(0, wdil_ref.shape[1], sub):
            acc = jnp.dot(xnp, wdil_ref[:, s:s + sub], preferred_element_type=F32)
            for r in range(MAX_DIL):
                dil_ref[r, ci * per:(ci + 1) * per, s:s + sub] = acc[r * per:(r + 1) * per, :]


def _in_proj(x, g, w_fox, w_dil, w_f, fb, perm, layer, B, S, *, tm=REGROUP, sub=IN_PROJ_TN):
    T, D = x.shape
    n_fox = 3 * D_FOX
    n_dil = 3 * D_DIL
    per_b = S // tm
    once = pl.Buffered(1)
    kern = functools.partial(_in_proj_kernel, sub=sub)
    return pl.pallas_call(
        kern,
        grid=(T // tm,),
        in_specs=[
            pl.BlockSpec((tm, D), lambda i: (i, 0)),
            pl.BlockSpec((None, 1, D), lambda i: (layer, 0, 0)),
            pl.BlockSpec((None, D, n_fox), lambda i: (layer, 0, 0), pipeline_mode=once),
            pl.BlockSpec((None, D, n_dil), lambda i: (layer, 0, 0), pipeline_mode=once),
            pl.BlockSpec((None, D, LANES), lambda i: (layer, 0, 0)),
            pl.BlockSpec((None, 1, LANES), lambda i: (layer, 0, 0)),
            pl.BlockSpec((REGROUP, REGROUP), lambda i: (0, 0)),
        ],
        out_specs=[
            pl.BlockSpec((tm, n_fox), lambda i: (i, 0)),
            pl.BlockSpec((None, MAX_DIL, tm // MAX_DIL, n_dil), lambda i: (i // per_b, 0, i % per_b, 0)),
            pl.BlockSpec((tm, LANES), lambda i: (i, 0)),
        ],
        out_shape=[
            jax.ShapeDtypeStruct((T, n_fox), BF16),
            jax.ShapeDtypeStruct((B, MAX_DIL, S // MAX_DIL, n_dil), F32),
            jax.ShapeDtypeStruct((T, LANES), F32),
        ],
        compiler_params=_cparams(("parallel",)),
        name="in_proj",
    )(x, g, w_fox, w_dil, w_f, fb, perm)


def _split3(x):
    hi = x.astype(BF16)
    r1 = x - hi.astype(F32)
    mid = r1.astype(BF16)
    lo = (r1 - mid.astype(F32)).astype(BF16)
    return hi, mid, lo


def _cumsum_kernel(logf_ref, cs_ref, *, chunk):
    S = logf_ref.shape[0]
    n_chunks = S // chunk
    row = lax.broadcasted_iota(jnp.int32, (chunk, chunk), 0)
    col = lax.broadcasted_iota(jnp.int32, (chunk, chunk), 1)
    tri = (col <= row).astype(BF16)
    carry = jnp.zeros((1, LANES), F32)
    for ci in range(n_chunks):
        seg = logf_ref[ci * chunk:(ci + 1) * chunk, :]
        hi, mid, lo = _split3(seg)
        within = (jnp.dot(tri, hi, preferred_element_type=F32)
                  + jnp.dot(tri, mid, preferred_element_type=F32)
                  + jnp.dot(tri, lo, preferred_element_type=F32))
        within = within + carry
        cs_ref[ci * chunk:(ci + 1) * chunk, :] = within
        carry = within[chunk - 1:chunk, :]


def _fox_cumsum(logf, B, S):
    kern = functools.partial(_cumsum_kernel, chunk=128)
    return pl.pallas_call(
        kern,
        grid=(B,),
        in_specs=[pl.BlockSpec((S, LANES), lambda b: (b, 0))],
        out_specs=pl.BlockSpec((S, LANES), lambda b: (b, 0)),
        out_shape=jax.ShapeDtypeStruct((B * S, LANES), F32),
        compiler_params=_cparams(("parallel",)),
        name="fox_cumsum",
    )(logf)


def _fox_head_kernel(q_ref, k_ref, v_ref, cs_ref, o_ref, kaug_ref, vaug_ref, qaug_ref, s0_ref,
                     s1_ref, p0_ref, p1_ref, a0_ref, a1_ref, m_ref, accl_ref, *, tq, scale):
    h = pl.program_id(1)
    S = k_ref.shape[0]
    tk = tq // 2
    scale2 = scale * LOG2E
    nt = (((1,), (1,)), ((), ()))
    s_refs = (s0_ref, s1_ref)
    pa_refs = ((p0_ref, a0_ref), (p1_ref, a1_ref))

    lane = lax.broadcasted_iota(jnp.int32, (tk, LANES), 1)
    for ci in range(S // tk):
        rows = slice(ci * tk, (ci + 1) * tk)
        col = jnp.sum(jnp.where(lane == h, cs_ref[rows, :], 0.0), axis=-1, keepdims=True)
        hi, mid, lo = _split3(col * (-1.0 / scale))
        aug = jnp.where(lane == 0, hi.astype(F32),
                        jnp.where(lane == 1, mid.astype(F32),
                                  jnp.where(lane == 2, lo.astype(F32), 0.0)))
        kaug_ref[rows, :HEAD_DIM] = k_ref[rows, :]
        kaug_ref[rows, HEAD_DIM:] = aug.astype(BF16)
        vaug_ref[rows, :HEAD_DIM] = v_ref[rows, :]
        vaug_ref[rows, HEAD_DIM:] = jnp.ones((tk, LANES), BF16)
        qaug_ref[rows, :HEAD_DIM] = q_ref[rows, :]
        qaug_ref[rows, HEAD_DIM:] = jnp.where(lane < 3, 1.0, 0.0).astype(BF16)

    def scores(q0, blk, s_ref, row_lo):
        s_ref[row_lo:, :] = lax.dot_general(qaug_ref[q0 + row_lo:q0 + tq, :],
                                            kaug_ref[blk * tk:(blk + 1) * tk, :], nt,
                                            preferred_element_type=F32)

    def pv_update(blk, row_lo, first, p_ref, alpha_ref):
        oa = jnp.dot(p_ref[row_lo:, :], vaug_ref[blk * tk:(blk + 1) * tk, :],
                     preferred_element_type=F32)
        if first:
            accl_ref[...] = oa
        else:
            a = alpha_ref[row_lo:, :]
            accl_ref[row_lo:, :HEAD_DIM] = a * accl_ref[row_lo:, :HEAD_DIM] + oa[:, :HEAD_DIM]
            accl_ref[row_lo:, HEAD_DIM:] = a * accl_ref[row_lo:, HEAD_DIM:] + oa[:, HEAD_DIM:]

    def softmax(s_ref, row_lo, row_hi, masked, first, p_ref, alpha_ref):
        rows = slice(row_lo, row_hi)
        n = row_hi - row_lo
        tiles = [s_ref[rows, j * LANES:(j + 1) * LANES] for j in range(tk // LANES)]
        if masked:
            row = lax.broadcasted_iota(jnp.int32, (n, LANES), 0)
            col = lax.broadcasted_iota(jnp.int32, (n, LANES), 1)
            tiles = [jnp.where(col + j * LANES <= row, t, NEG_INF) for j, t in enumerate(tiles)]
        top = jnp.max(functools.reduce(jnp.maximum, tiles), axis=-1, keepdims=True) * scale2
        if first:
            m_new = jnp.maximum(jnp.full((n, LANES), NEG_INF, F32), top)
        else:
            m_prev = m_ref[rows, :]
            m_new = jnp.maximum(m_prev, top)
            alpha_ref[rows, :] = jnp.exp2(m_prev - m_new)
        m_ref[rows, :] = m_new
        for j, t in enumerate(tiles):
            p_ref[rows, j * LANES:(j + 1) * LANES] = jnp.exp2(t * scale2 - m_new).astype(BF16)

    for qi in range(S // tq):
        q0 = qi * tq
        blocks = [(b, 0, None) for b in range(2 * qi)] + [(2 * qi, 0, (0, tk)), (2 * qi + 1, tk, (tk, tq))]
        scores(q0, blocks[0][0], s_refs[0], blocks[0][1])
        for idx, (blk, row_lo, masked) in enumerate(blocks):
            if idx + 1 < len(blocks):
                scores(q0, blocks[idx + 1][0], s_refs[(idx + 1) % 2], blocks[idx + 1][1])
            if idx >= 1:
                pv_update(blocks[idx - 1][0], blocks[idx - 1][1], idx == 1, *pa_refs[(idx - 1) % 2])
            s_ref = s_refs[idx % 2]
            if masked is None:
                softmax(s_ref, 0, tq, False, idx == 0, *pa_refs[idx % 2])
            else:
                softmax(s_ref, masked[0], masked[1], True, idx == 0, *pa_refs[idx % 2])
                if masked[1] < tq:
                    softmax(s_ref, masked[1], tq, False, idx == 0, *pa_refs[idx % 2])
        pv_update(blocks[-1][0], blocks[-1][1], False, *pa_refs[(len(blocks) - 1) % 2])
        o_ref[q0:q0 + tq, :] = accl_ref[:, :HEAD_DIM] / accl_ref[:, HEAD_DIM:]


def _fox_attention(fox, cs, B, S, *, tq=1024):
    H = N_HEADS_FOX
    tk = tq // 2
    kern = functools.partial(_fox_head_kernel, tq=tq, scale=HEAD_DIM ** -0.5)
    return pl.pallas_call(
        kern,
        grid=(B, H),
        in_specs=[
            pl.BlockSpec((None, S, HEAD_DIM), lambda b, h: (b, 0, h)),
            pl.BlockSpec((None, S, HEAD_DIM), lambda b, h: (b, 0, H + h)),
            pl.BlockSpec((None, S, HEAD_DIM), lambda b, h: (b, 0, 2 * H + h)),
            pl.BlockSpec((None, S, LANES), lambda b, h: (b, 0, 0)),
        ],
        out_specs=pl.BlockSpec((None, S, HEAD_DIM), lambda b, h: (b, 0, h)),
        out_shape=jax.ShapeDtypeStruct((B, S, D_FOX), F32),
        scratch_shapes=[
            pltpu.VMEM((S, 2 * HEAD_DIM), BF16),
            pltpu.VMEM((S, 2 * HEAD_DIM), BF16),
            pltpu.VMEM((S, 2 * HEAD_DIM), BF16),
            pltpu.VMEM((tq, tk), F32),
            pltpu.VMEM((tq, tk), F32),
            pltpu.VMEM((tq, tk), BF16),
            pltpu.VMEM((tq, tk), BF16),
            pltpu.VMEM((tq, LANES), F32),
            pltpu.VMEM((tq, LANES), F32),
            pltpu.VMEM((tq, LANES), F32),
            pltpu.VMEM((tq, 2 * HEAD_DIM), F32),
        ],
        compiler_params=_cparams(("parallel", "parallel")),
        name="fox_attention",
    )(fox, fox, fox, cs)


def _rel_bucket_np(dist):
    max_exact = REL_BUCKETS // 2
    d = np.maximum(dist.astype(np.float32), np.float32(1.0))
    ratio = np.log(d / np.float32(max_exact)).astype(np.float32) / np.float32(
        math.log(REL_MAX_DISTANCE / max_exact))
    large = max_exact + (ratio * np.float32(REL_BUCKETS - max_exact)).astype(np.int32)
    large = np.minimum(large, REL_BUCKETS - 1)
    return np.where(dist < max_exact, dist, large)


def _block_perm(dilation):
    n_chunks = MAX_DIL // dilation
    rows = DIL_BLOCK // n_chunks
    a = np.arange(n_chunks)[:, None]
    nn = np.arange(rows)[None, :]
    return (nn * n_chunks + a).reshape(-1)


def _bucket_tables():
    tabs = []
    for window, dilation in DIL_PATTERNS:
        span = window // dilation
        perm = _block_perm(dilation)
        i = perm[:, None]
        j = np.concatenate([perm, DIL_BLOCK + perm])[None, :]
        rel = DIL_BLOCK + i - j
        in_band = (rel >= 0) & (rel <= span)
        bucket = _rel_bucket_np(np.clip(rel, 0, span) * dilation)
        full = np.where(in_band, bucket, -1).astype(np.int32)
        first = np.where(j >= DIL_BLOCK, full, -1).astype(np.int32)
        tabs.append(np.stack([full, first]))
    return np.stack(tabs)


def _bias_table_kernel(rel_bias_ref, idx_ref, tab_ref):
    idx = idx_ref[...]
    for h in range(tab_ref.shape[0]):
        tab = jnp.full(idx.shape, NEG_INF, F32)
        for b in range(REL_BUCKETS):
            tab = jnp.where(idx == b, rel_bias_ref[b, h] * LOG2E, tab)
        tab_ref[h] = tab


def _bias_tables(rel_bias):
    idx = jnp.asarray(_bucket_tables())
    P = len(DIL_PATTERNS)
    blk = (2, DIL_BLOCK, 2 * DIL_BLOCK)
    return pl.pallas_call(
        _bias_table_kernel,
        grid=(P,),
        in_specs=[
            pl.BlockSpec(memory_space=pltpu.SMEM),
            pl.BlockSpec((None,) + blk, lambda p: (p, 0, 0, 0)),
        ],
        out_specs=pl.BlockSpec((None, N_HEADS_DIL) + blk, lambda p: (p, 0, 0, 0, 0)),
        out_shape=jax.ShapeDtypeStruct((P, N_HEADS_DIL) + blk, F32),
        compiler_params=_cparams(("parallel",)),
        name="dil_bias_tables",
    )(rel_bias, idx)


def _dil_kernel(q_ref, k_ref, v_ref, tab_ref, o_ref, acc_ref, l_ref, m_ref, *, scale, group):
    n_sub = q_ref.shape[1]
    n_pat = len(DIL_PATTERNS)
    scale2 = scale * LOG2E
    ones = jnp.ones((2 * DIL_BLOCK, HEAD_DIM), BF16)

    def rows_at(start, rows):
        return pl.ds(start if isinstance(start, int) else pl.multiple_of(start, rows), rows)

    def gather(ref, dilation, r0, starts):
        n_chunks = MAX_DIL // dilation
        rows = DIL_BLOCK // n_chunks
        parts = [ref[r0 + a * dilation, rows_at(s, rows), :]
                 for s in starts for a in range(n_chunks)]
        return parts[0] if len(parts) == 1 else jnp.concatenate(parts, axis=0)

    def scatter(ref, val, dilation, r0, start):
        n_chunks = MAX_DIL // dilation
        rows = DIL_BLOCK // n_chunks
        for a in range(n_chunks):
            ref[r0 + a * dilation, rows_at(start, rows), :] = val[a * rows:(a + 1) * rows]

    def block(p_idx, dilation, r0, blk):
        rows = DIL_BLOCK * dilation // MAX_DIL
        cur = blk * rows
        q = gather(q_ref, dilation, r0, [cur]).astype(BF16)
        if isinstance(blk, int) and blk == 0:
            starts = [cur]
            tab = tab_ref[p_idx, 1, :, DIL_BLOCK:]
        elif isinstance(blk, int):
            starts = [cur - rows, cur]
            tab = tab_ref[p_idx, 0]
        else:
            starts = [jnp.maximum(blk - 1, 0) * rows, cur]
            tab = tab_ref[p_idx, (blk == 0).astype(jnp.int32)]
        k = gather(k_ref, dilation, r0, starts).astype(BF16)
        v = gather(v_ref, dilation, r0, starts).astype(BF16)
        s2 = lax.dot_general(q, k, (((1,), (1,)), ((), ())), preferred_element_type=F32)
        s2 = s2 * scale2 + tab
        m = jnp.max(s2, axis=-1, keepdims=True)
        p = jnp.exp2(s2 - m).astype(BF16)
        ol = jnp.dot(p, jnp.concatenate([v, ones[:v.shape[0]]], axis=1), preferred_element_type=F32)
        scatter(acc_ref.at[p_idx], ol[:, :HEAD_DIM], dilation, r0, cur)
        scatter(l_ref.at[p_idx], ol[:, HEAD_DIM:], dilation, r0, cur)
        scatter(m_ref.at[p_idx], jnp.broadcast_to(m, (DIL_BLOCK, HEAD_DIM)), dilation, r0, cur)

    for p_idx, (_, dilation) in enumerate(DIL_PATTERNS):
        n_blocks = (n_sub * MAX_DIL // dilation) // DIL_BLOCK
        if n_blocks <= 2:
            per_iter = group // n_blocks

            def body(g, carry, p_idx=p_idx, dilation=dilation, n_blocks=n_blocks, per_iter=per_iter):
                for i in range(per_iter):
                    for blk in range(n_blocks):
                        block(p_idx, dilation, g * per_iter + i, blk)
                return carry

            lax.fori_loop(0, dilation // per_iter, body, 0)
        else:
            per_iter = max(group // dilation, 1)

            def body(g, carry, p_idx=p_idx, dilation=dilation, per_iter=per_iter):
                for i in range(per_iter):
                    for r0 in range(dilation):
                        block(p_idx, dilation, r0, g * per_iter + i)
                return carry

            lax.fori_loop(0, n_blocks // per_iter, body, 0)

    chunk = DIL_BLOCK

    def combine(t, carry):
        r = t // (n_sub // chunk)
        rows = pl.ds(pl.multiple_of((t % (n_sub // chunk)) * chunk, chunk), chunk)
        ms = [m_ref[p, r, rows, :] for p in range(n_pat)]
        top = functools.reduce(jnp.maximum, ms)
        w = [jnp.exp2(x - top) for x in ms]
        num = functools.reduce(lambda a, b: a + b, [w[p] * acc_ref[p, r, rows, :] for p in range(n_pat)])
        den = functools.reduce(lambda a, b: a + b, [w[p] * l_ref[p, r, rows, :] for p in range(n_pat)])
        o_ref[r, rows, :] = num / den
        return carry

    lax.fori_loop(0, MAX_DIL * (n_sub // chunk), combine, 0)


def _dil_attention(qkv, tabs, B, S, *, group=32):
    H = N_HEADS_DIL
    n_sub = S // MAX_DIL
    P = len(DIL_PATTERNS)
    kern = functools.partial(_dil_kernel, scale=HEAD_DIM ** -0.5, group=group)
    blk = (None, MAX_DIL, n_sub, HEAD_DIM)
    return pl.pallas_call(
        kern,
        grid=(B, H),
        in_specs=[
            pl.BlockSpec(blk, lambda b, h: (b, 0, 0, h)),
            pl.BlockSpec(blk, lambda b, h: (b, 0, 0, H + h)),
            pl.BlockSpec(blk, lambda b, h: (b, 0, 0, 2 * H + h)),
            pl.BlockSpec((P, None, 2, DIL_BLOCK, 2 * DIL_BLOCK), lambda b, h: (0, h, 0, 0, 0)),
        ],
        out_specs=pl.BlockSpec(blk, lambda b, h: (b, 0, 0, h)),
        out_shape=jax.ShapeDtypeStruct((B, MAX_DIL, n_sub, D_DIL), F32),
        scratch_shapes=[pltpu.VMEM((P, MAX_DIL, n_sub, HEAD_DIM), F32)] * 3,
        compiler_params=_cparams(("parallel", "parallel")),
        name="dil_attention",
    )(qkv, qkv, qkv, tabs)


def _out_proj_kernel(ya_ref, yb_ref, ga_ref, gb_ref, perm_ref, w_ref, x_ref, o_ref):
    tm = ya_ref.shape[0]
    per = REGROUP // MAX_DIL
    for c in range(tm // REGROUP):
        rows = slice(c * REGROUP, (c + 1) * REGROUP)
        ya = ya_ref[rows, :]
        mix_a = (ya * _rms_scale(ya) * ga_ref[...]).astype(BF16)
        yb = jnp.concatenate([yb_ref[r, c * per:(c + 1) * per, :] for r in range(MAX_DIL)], axis=0)
        mix_b = (yb * _rms_scale(yb) * gb_ref[...]).astype(BF16)
        mix_b = jnp.dot(perm_ref[...], mix_b, preferred_element_type=F32).astype(BF16)
        mix = jnp.concatenate([mix_a, mix_b], axis=1)
        o_ref[rows, :] = x_ref[rows, :] + jnp.dot(mix, w_ref[...], preferred_element_type=F32)


def _out_proj(ya, yb, ga, gb, perm, w, x, layer, B, S, *, tm=512):
    T, D = x.shape
    per_b = S // tm
    return pl.pallas_call(
        _out_proj_kernel,
        grid=(T // tm,),
        in_specs=[
            pl.BlockSpec((tm, D_FOX), lambda i: (i, 0)),
            pl.BlockSpec((None, MAX_DIL, tm // MAX_DIL, D_DIL), lambda i: (i // per_b, 0, i % per_b, 0)),
            pl.BlockSpec((None, 1, D_FOX), lambda i: (layer, 0, 0)),
            pl.BlockSpec((None, 1, D_DIL), lambda i: (layer, 0, 0)),
            pl.BlockSpec((REGROUP, REGROUP), lambda i: (0, 0)),
            pl.BlockSpec((None, D_FOX + D_DIL, D), lambda i: (layer, 0, 0)),
            pl.BlockSpec((tm, D), lambda i: (i, 0)),
        ],
        out_specs=pl.BlockSpec((tm, D), lambda i: (i, 0)),
        out_shape=jax.ShapeDtypeStruct((T, D), F32),
        compiler_params=_cparams(("parallel",)),
        name="out_proj",
    )(ya, yb, ga, gb, perm, w, x)


def _mlp_kernel(x_ref, g_ref, w1_ref, w2_ref, fg_ref, o_ref, xn_ref, acc_ref, *, final_norm, sub):
    f = pl.program_id(1)
    tf = w1_ref.shape[1]

    @pl.when(f == 0)
    def _():
        x = x_ref[...]
        xn_ref[...] = (x * _rms_scale(x) * g_ref[...]).astype(BF16)
        acc_ref[...] = x

    upd = None
    for c in range(tf // sub):
        h = jnp.dot(xn_ref[...], w1_ref[:, c * sub:(c + 1) * sub], preferred_element_type=F32)
        h = jnp.square(jnp.maximum(h, 0.0)).astype(BF16)
        d = jnp.dot(h, w2_ref[c * sub:(c + 1) * sub, :], preferred_element_type=F32)
        upd = d if upd is None else upd + d
    acc_ref[...] += upd

    @pl.when(f == pl.num_programs(1) - 1)
    def _():
        y = acc_ref[...]
        if final_norm:
            y = y * _rms_scale(y) * fg_ref[...]
        o_ref[...] = y


def _mlp(x, g, w1, w2, fg, layer, *, final_norm, tm=512, tf=MLP_TF, sub=512):
    T, D = x.shape
    F = w1.shape[2]
    kern = functools.partial(_mlp_kernel, final_norm=final_norm, sub=sub)
    return pl.pallas_call(
        kern,
        grid=(T // tm, F // tf),
        in_specs=[
            pl.BlockSpec((tm, D), lambda i, f: (i, 0)),
            pl.BlockSpec((None, 1, D), lambda i, f: (layer, 0, 0)),
            pl.BlockSpec((None, D, tf), lambda i, f: (layer, 0, f)),
            pl.BlockSpec((None, tf, D), lambda i, f: (layer, f, 0)),
            pl.BlockSpec((1, D), lambda i, f: (0, 0)),
        ],
        out_specs=pl.BlockSpec((tm, D), lambda i, f: (i, 0)),
        out_shape=jax.ShapeDtypeStruct((T, D), F32),
        scratch_shapes=[pltpu.VMEM((tm, D), BF16), pltpu.VMEM((tm, D), F32)],
        compiler_params=_cparams(("parallel", "arbitrary")),
        name="mlp",
    )(x, g, w1, w2, fg)


def kernel(x, norm1_g, w_in, forget_b, rel_bias, outnorm_a_g, outnorm_b_g, w_out,
           norm2_g, w_mlp_in, w_mlp_out, final_norm_g):
    B, S, D = x.shape
    T = B * S
    depth = w_in.shape[0]
    f0 = 3 * D_FOX
    assert S % (MAX_DIL * DIL_BLOCK) == 0 and w_in.shape[2] == f0 + N_HEADS_FOX + 3 * D_DIL

    w_fox = w_in[:, :, :f0].astype(BF16)
    w_dil = w_in[:, :, f0 + N_HEADS_FOX:].astype(BF16)
    w_f = jnp.pad(w_in[:, :, f0:f0 + N_HEADS_FOX], ((0, 0), (0, 0), (0, LANES - N_HEADS_FOX))).astype(BF16)
    fb = jnp.pad(forget_b.astype(F32), ((0, 0), (0, LANES - N_HEADS_FOX))).reshape(depth, 1, LANES)
    w_o = w_out.astype(BF16)
    w1 = w_mlp_in.astype(BF16)
    w2 = w_mlp_out.astype(BF16)
    g1 = norm1_g.astype(F32).reshape(depth, 1, D)
    g2 = norm2_g.astype(F32).reshape(depth, 1, D)
    ga = outnorm_a_g.astype(F32).reshape(depth, 1, D_FOX)
    gb = outnorm_b_g.astype(F32).reshape(depth, 1, D_DIL)
    fg = final_norm_g.astype(F32).reshape(1, D)
    perm = _regroup_matrix()
    tabs = _bias_tables(rel_bias.astype(F32))

    xf = x.reshape(T, D)
    for l in range(depth):
        fox, dil, logf = _in_proj(xf, g1, w_fox, w_dil, w_f, fb, perm, l, B, S)
        cs = _fox_cumsum(logf, B, S)
        ya = _fox_attention(fox.reshape(B, S, 3 * D_FOX), cs.reshape(B, S, LANES), B, S)
        yb = _dil_attention(dil, tabs, B, S)
        xf = _out_proj(ya.reshape(T, D_FOX), yb, ga, gb, perm, w_o, xf, l, B, S)
        xf = _mlp(xf, g2, w1, w2, fg, l, final_norm=(l == depth - 1))
    return xf.reshape(B, S, D)
```

```python
import functools
import math

import numpy as np
import jax
import jax.numpy as jnp
from jax import lax
from jax.experimental import pallas as pl
from jax.experimental.pallas import tpu as pltpu

F32 = jnp.float32
BF16 = jnp.bfloat16

HEAD_DIM = 128
N_HEADS_FOX = 8
N_HEADS_DIL = 8
D_FOX = N_HEADS_FOX * HEAD_DIM
D_DIL = N_HEADS_DIL * HEAD_DIM
DIL_PATTERNS = ((128, 1), (512, 4), (2048, 16))
DIL_BLOCK = 128
MAX_DIL = 16
REGROUP = MAX_DIL * MAX_DIL
REL_BUCKETS = 32
REL_MAX_DISTANCE = 2048
NORM_EPS = 1e-6
NEG_INF = -1e30
LOG2E = math.log2(math.e)

LANES = 128
IN_PROJ_TN = 512
MLP_TF = 1024
VMEM_LIMIT = 56 * 1024 * 1024


def _cparams(sem, flags=None):
    return pltpu.CompilerParams(dimension_semantics=sem, vmem_limit_bytes=VMEM_LIMIT, flags=flags)


def _rms_scale(x):
    return lax.rsqrt(jnp.mean(x * x, axis=-1, keepdims=True) + NORM_EPS)


def _regroup_matrix():
    t = np.arange(REGROUP)
    src = (t % MAX_DIL) * MAX_DIL + t // MAX_DIL
    return jnp.asarray(np.eye(REGROUP, dtype=np.float32)[src], dtype=BF16)


def _in_proj_kernel(x_ref, g_ref, wfox_ref, wdil_ref, wf_ref, fb_ref, perm_ref, fox_ref, dil_ref,
                    logf_ref, *, sub):
    tm = x_ref.shape[0]
    per = REGROUP // MAX_DIL
    for ci in range(tm // REGROUP):
        rows = slice(ci * REGROUP, (ci + 1) * REGROUP)
        x = x_ref[rows, :]
        xn = (x * _rms_scale(x) * g_ref[...]).astype(BF16)
        xnp = jnp.dot(perm_ref[...], xn, preferred_element_type=F32).astype(BF16)
        f = jnp.dot(xn, wf_ref[...], preferred_element_type=F32) + fb_ref[...]
        logf_ref[rows, :] = jnp.minimum(f, 0.0) - jnp.log1p(jnp.exp(-jnp.abs(f)))
        for s in range(0, wfox_ref.shape[1], sub):
            fox_ref[rows, s:s + sub] = jnp.dot(xn, wfox_ref[:, s:s + sub],
                                               preferred_element_type=F32).astype(BF16)
        for s in range(0, wdil_ref.shape[1], sub):
            acc = jnp.dot(xnp, wdil_ref[:, s:s + sub], preferred_element_type=F32)
            for r in range(MAX_DIL):
                dil_ref[r, ci * per:(ci + 1) * per, s:s + sub] = acc[r * per:(r + 1) * per, :]


def _in_proj(x, g, w_fox, w_dil, w_f, fb, perm, layer, B, S, *, tm=REGROUP, sub=IN_PROJ_TN):
    T, D = x.shape
    n_fox = 3 * D_FOX
    n_dil = 3 * D_DIL
    per_b = S // tm
    once = pl.Buffered(1)
    kern = functools.partial(_in_proj_kernel, sub=sub)
    return pl.pallas_call(
        kern,
        grid=(T // tm,),
        in_specs=[
            pl.BlockSpec((tm, D), lambda i: (i, 0)),
            pl.BlockSpec((None, 1, D), lambda i: (layer, 0, 0)),
            pl.BlockSpec((None, D, n_fox), lambda i: (layer, 0, 0), pipeline_mode=once),
            pl.BlockSpec((None, D, n_dil), lambda i: (layer, 0, 0), pipeline_mode=once),
            pl.BlockSpec((None, D, LANES), lambda i: (layer, 0, 0)),
            pl.BlockSpec((None, 1, LANES), lambda i: (layer, 0, 0)),
            pl.BlockSpec((REGROUP, REGROUP), lambda i: (0, 0)),
        ],
        out_specs=[
            pl.BlockSpec((tm, n_fox), lambda i: (i, 0)),
            pl.BlockSpec((None, MAX_DIL, tm // MAX_DIL, n_dil), lambda i: (i // per_b, 0, i % per_b, 0)),
            pl.BlockSpec((tm, LANES), lambda i: (i, 0)),
        ],
        out_shape=[
            jax.ShapeDtypeStruct((T, n_fox), BF16),
            jax.ShapeDtypeStruct((B, MAX_DIL, S // MAX_DIL, n_dil), F32),
            jax.ShapeDtypeStruct((T, LANES), F32),
        ],
        compiler_params=_cparams(("parallel",)),
        name="in_proj",
    )(x, g, w_fox, w_dil, w_f, fb, perm)


def _split3(x):
    hi = x.astype(BF16)
    r1 = x - hi.astype(F32)
    mid = r1.astype(BF16)
    lo = (r1 - mid.astype(F32)).astype(BF16)
    return hi, mid, lo


def _cumsum_kernel(logf_ref, cs_ref, *, chunk):
    S = logf_ref.shape[0]
    n_chunks = S // chunk
    row = lax.broadcasted_iota(jnp.int32, (chunk, chunk), 0)
    col = lax.broadcasted_iota(jnp.int32, (chunk, chunk), 1)
    tri = (col <= row).astype(BF16)
    carry = jnp.zeros((1, LANES), F32)
    for ci in range(n_chunks):
        seg = logf_ref[ci * chunk:(ci + 1) * chunk, :]
        hi, mid, lo = _split3(seg)
        within = (jnp.dot(tri, hi, preferred_element_type=F32)
                  + jnp.dot(tri, mid, preferred_element_type=F32)
                  + jnp.dot(tri, lo, preferred_element_type=F32))
        within = within + carry
        cs_ref[ci * chunk:(ci + 1) * chunk, :] = within
        carry = within[chunk - 1:chunk, :]


def _fox_cumsum(logf, B, S):
    kern = functools.partial(_cumsum_kernel, chunk=128)
    return pl.pallas_call(
        kern,
        grid=(B,),
        in_specs=[pl.BlockSpec((S, LANES), lambda b: (b, 0))],
        out_specs=pl.BlockSpec((S, LANES), lambda b: (b, 0)),
        out_shape=jax.ShapeDtypeStruct((B * S, LANES), F32),
        compiler_params=_cparams(("parallel",)),
        name="fox_cumsum",
    )(logf)


def _fox_head_kernel(q_ref, k_ref, v_ref, cs_ref, o_ref, kaug_ref, vaug_ref, qaug_ref, s0_ref,
                     s1_ref, p0_ref, p1_ref, a0_ref, a1_ref, m_ref, accl_ref, *, tq, scale):
    h = pl.program_id(1)
    S = k_ref.shape[0]
    tk = tq // 2
    scale2 = scale * LOG2E
    nt = (((1,), (1,)), ((), ()))
    s_refs = (s0_ref, s1_ref)
    pa_refs = ((p0_ref, a0_ref), (p1_ref, a1_ref))

    lane = lax.broadcasted_iota(jnp.int32, (tk, LANES), 1)
    for ci in range(S // tk):
        rows = slice(ci * tk, (ci + 1) * tk)
        col = jnp.sum(jnp.where(lane == h, cs_ref[rows, :], 0.0), axis=-1, keepdims=True)
        hi, mid, lo = _split3(col * (-1.0 / scale))
        aug = jnp.where(lane == 0, hi.astype(F32),
                        jnp.where(lane == 1, mid.astype(F32),
                                  jnp.where(lane == 2, lo.astype(F32), 0.0)))
        kaug_ref[rows, :HEAD_DIM] = k_ref[rows, :]
        kaug_ref[rows, HEAD_DIM:] = aug.astype(BF16)
        vaug_ref[rows, :HEAD_DIM] = v_ref[rows, :]
        vaug_ref[rows, HEAD_DIM:] = jnp.ones((tk, LANES), BF16)
        qaug_ref[rows, :HEAD_DIM] = q_ref[rows, :]
        qaug_ref[rows, HEAD_DIM:] = jnp.where(lane < 3, 1.0, 0.0).astype(BF16)

    def scores(q0, blk, s_ref, row_lo):
        s_ref[row_lo:, :] = lax.dot_general(qaug_ref[q0 + row_lo:q0 + tq, :],
                                            kaug_ref[blk * tk:(blk + 1) * tk, :], nt,
                                            preferred_element_type=F32)

    def pv_update(blk, row_lo, first, p_ref, alpha_ref):
        oa = jnp.dot(p_ref[row_lo:, :], vaug_ref[blk * tk:(blk + 1) * tk, :],
                     preferred_element_type=F32)
        if first:
            accl_ref[...] = oa
        else:
            a = alpha_ref[row_lo:, :]
            accl_ref[row_lo:, :HEAD_DIM] = a * accl_ref[row_lo:, :HEAD_DIM] + oa[:, :HEAD_DIM]
            accl_ref[row_lo:, HEAD_DIM:] = a * accl_ref[row_lo:, HEAD_DIM:] + oa[:, HEAD_DIM:]

    def softmax(s_ref, row_lo, row_hi, masked, first, p_ref, alpha_ref):
        rows = slice(row_lo, row_hi)
        n = row_hi - row_lo
        tiles = [s_ref[rows, j * LANES:(j + 1) * LANES] for j in range(tk // LANES)]
        if masked:
            row = lax.broadcasted_iota(jnp.int32, (n, LANES), 0)
            col = lax.broadcasted_iota(jnp.int32, (n, LANES), 1)
            tiles = [jnp.where(col + j * LANES <= row, t, NEG_INF) for j, t in enumerate(tiles)]
        top = jnp.max(functools.reduce(jnp.maximum, tiles), axis=-1, keepdims=True) * scale2
        if first:
            m_new = jnp.maximum(jnp.full((n, LANES), NEG_INF, F32), top)
        else:
            m_prev = m_ref[rows, :]
            m_new = jnp.maximum(m_prev, top)
            alpha_ref[rows, :] = jnp.exp2(m_prev - m_new)
        m_ref[rows, :] = m_new
        for j, t in enumerate(tiles):
            p_ref[rows, j * LANES:(j + 1) * LANES] = jnp.exp2(t * scale2 - m_new).astype(BF16)

    for qi in range(S // tq):
        q0 = qi * tq
        blocks = [(b, 0, None) for b in range(2 * qi)] + [(2 * qi, 0, (0, tk)), (2 * qi + 1, tk, (tk, tq))]
        scores(q0, blocks[0][0], s_refs[0], blocks[0][1])
        for idx, (blk, row_lo, masked) in enumerate(blocks):
            if idx + 1 < len(blocks):
                scores(q0, blocks[idx + 1][0], s_refs[(idx + 1) % 2], blocks[idx + 1][1])
            if idx >= 1:
                pv_update(blocks[idx - 1][0], blocks[idx - 1][1], idx == 1, *pa_refs[(idx - 1) % 2])
            s_ref = s_refs[idx % 2]
            if masked is None:
                softmax(s_ref, 0, tq, False, idx == 0, *pa_refs[idx % 2])
            else:
                softmax(s_ref, masked[0], masked[1], True, idx == 0, *pa_refs[idx % 2])
                if masked[1] < tq:
                    softmax(s_ref, masked[1], tq, False, idx == 0, *pa_refs[idx % 2])
        pv_update(blocks[-1][0], blocks[-1][1], False, *pa_refs[(len(blocks) - 1) % 2])
        o_ref[q0:q0 + tq, :] = accl_ref[:, :HEAD_DIM] / accl_ref[:, HEAD_DIM:]


def _fox_attention(fox, cs, B, S, *, tq=1024):
    H = N_HEADS_FOX
    tk = tq // 2
    kern = functools.partial(_fox_head_kernel, tq=tq, scale=HEAD_DIM ** -0.5)
    return pl.pallas_call(
        kern,
        grid=(B, H),
        in_specs=[
            pl.BlockSpec((None, S, HEAD_DIM), lambda b, h: (b, 0, h)),
            pl.BlockSpec((None, S, HEAD_DIM), lambda b, h: (b, 0, H + h)),
            pl.BlockSpec((None, S, HEAD_DIM), lambda b, h: (b, 0, 2 * H + h)),
            pl.BlockSpec((None, S, LANES), lambda b, h: (b, 0, 0)),
        ],
        out_specs=pl.BlockSpec((None, S, HEAD_DIM), lambda b, h: (b, 0, h)),
        out_shape=jax.ShapeDtypeStruct((B, S, D_FOX), F32),
        scratch_shapes=[
            pltpu.VMEM((S, 2 * HEAD_DIM), BF16),
            pltpu.VMEM((S, 2 * HEAD_DIM), BF16),
            pltpu.VMEM((S, 2 * HEAD_DIM), BF16),
            pltpu.VMEM((tq, tk), F32),
            pltpu.VMEM((tq, tk), F32),
            pltpu.VMEM((tq, tk), BF16),
            pltpu.VMEM((tq, tk), BF16),
            pltpu.VMEM((tq, LANES), F32),
            pltpu.VMEM((tq, LANES), F32),
            pltpu.VMEM((tq, LANES), F32),
            pltpu.VMEM((tq, 2 * HEAD_DIM), F32),
        ],
        compiler_params=_cparams(("parallel", "parallel")),
        name="fox_attention",
    )(fox, fox, fox, cs)


def _rel_bucket_np(dist):
    max_exact = REL_BUCKETS // 2
    d = np.maximum(dist.astype(np.float32), np.float32(1.0))
    ratio = np.log(d / np.float32(max_exact)).astype(np.float32) / np.float32(
        math.log(REL_MAX_DISTANCE / max_exact))
    large = max_exact + (ratio * np.float32(REL_BUCKETS - max_exact)).astype(np.int32)
    large = np.minimum(large, REL_BUCKETS - 1)
    return np.where(dist < max_exact, dist, large)


def _block_perm(dilation):
    n_chunks = MAX_DIL // dilation
    rows = DIL_BLOCK // n_chunks
    a = np.arange(n_chunks)[:, None]
    nn = np.arange(rows)[None, :]
    return (nn * n_chunks + a).reshape(-1)


def _bucket_tables():
    tabs = []
    for window, dilation in DIL_PATTERNS:
        span = window // dilation
        perm = _block_perm(dilation)
        i = perm[:, None]
        j = np.concatenate([perm, DIL_BLOCK + perm])[None, :]
        rel = DIL_BLOCK + i - j
        in_band = (rel >= 0) & (rel <= span)
        bucket = _rel_bucket_np(np.clip(rel, 0, span) * dilation)
        full = np.where(in_band, bucket, -1).astype(np.int32)
        first = np.where(j >= DIL_BLOCK, full, -1).astype(np.int32)
        tabs.append(np.stack([full, first]))
    return np.stack(tabs)


def _bias_table_kernel(rel_bias_ref, idx_ref, tab_ref):
    idx = idx_ref[...]
    for h in range(tab_ref.shape[0]):
        tab = jnp.full(idx.shape, NEG_INF, F32)
        for b in range(REL_BUCKETS):
            tab = jnp.where(idx == b, rel_bias_ref[b, h] * LOG2E, tab)
        tab_ref[h] = tab


def _bias_tables(rel_bias):
    idx = jnp.asarray(_bucket_tables())
    P = len(DIL_PATTERNS)
    blk = (2, DIL_BLOCK, 2 * DIL_BLOCK)
    return pl.pallas_call(
        _bias_table_kernel,
        grid=(P,),
        in_specs=[
            pl.BlockSpec(memory_space=pltpu.SMEM),
            pl.BlockSpec((None,) + blk, lambda p: (p, 0, 0, 0)),
        ],
        out_specs=pl.BlockSpec((None, N_HEADS_DIL) + blk, lambda p: (p, 0, 0, 0, 0)),
        out_shape=jax.ShapeDtypeStruct((P, N_HEADS_DIL) + blk, F32),
        compiler_params=_cparams(("parallel",)),
        name="dil_bias_tables",
    )(rel_bias, idx)


def _dil_kernel(q_ref, k_ref, v_ref, tab_ref, o_ref, acc_ref, l_ref, m_ref, *, scale, group):
    n_sub = q_ref.shape[1]
    n_pat = len(DIL_PATTERNS)
    scale2 = scale * LOG2E
    ones = jnp.ones((2 * DIL_BLOCK, HEAD_DIM), BF16)

    def rows_at(start, rows):
        return pl.ds(start if isinstance(start, int) else pl.multiple_of(start, rows), rows)

    def gather(ref, dilation, r0, starts):
        n_chunks = MAX_DIL // dilation
        rows = DIL_BLOCK // n_chunks
        parts = [ref[r0 + a * dilation, rows_at(s, rows), :]
                 for s in starts for a in range(n_chunks)]
        return parts[0] if len(parts) == 1 else jnp.concatenate(parts, axis=0)

    def scatter(ref, val, dilation, r0, start):
        n_chunks = MAX_DIL // dilation
        rows = DIL_BLOCK // n_chunks
        for a in range(n_chunks):
            ref[r0 + a * dilation, rows_at(start, rows), :] = val[a * rows:(a + 1) * rows]

    def block(p_idx, dilation, r0, blk):
        rows = DIL_BLOCK * dilation // MAX_DIL
        cur = blk * rows
        q = gather(q_ref, dilation, r0, [cur]).astype(BF16)
        if isinstance(blk, int) and blk == 0:
            starts = [cur]
            tab = tab_ref[p_idx, 1, :, DIL_BLOCK:]
        elif isinstance(blk, int):
            starts = [cur - rows, cur]
            tab = tab_ref[p_idx, 0]
        else:
            starts = [jnp.maximum(blk - 1, 0) * rows, cur]
            tab = tab_ref[p_idx, (blk == 0).astype(jnp.int32)]
        k = gather(k_ref, dilation, r0, starts).astype(BF16)
        v = gather(v_ref, dilation, r0, starts).astype(BF16)
        s2 = lax.dot_general(q, k, (((1,), (1,)), ((), ())), preferred_element_type=F32)
        s2 = s2 * scale2 + tab
        m = jnp.max(s2, axis=-1, keepdims=True)
        p = jnp.exp2(s2 - m).astype(BF16)
        ol = jnp.dot(p, jnp.concatenate([v, ones[:v.shape[0]]], axis=1), preferred_element_type=F32)
        if p_idx < n_pat - 1:
            scatter(acc_ref.at[p_idx], ol[:, :HEAD_DIM], dilation, r0, cur)
            scatter(l_ref.at[p_idx], ol[:, HEAD_DIM:], dilation, r0, cur)
            scatter(m_ref.at[p_idx], jnp.broadcast_to(m, (DIL_BLOCK, HEAD_DIM)), dilation, r0, cur)
        else:
            ms = [gather(m_ref.at[p], dilation, r0, [cur]) for p in range(n_pat - 1)] + [m]
            accs = [gather(acc_ref.at[p], dilation, r0, [cur]) for p in range(n_pat - 1)] + [ol[:, :HEAD_DIM]]
            ls = [gather(l_ref.at[p], dilation, r0, [cur]) for p in range(n_pat - 1)] + [ol[:, HEAD_DIM:]]
            top = functools.reduce(jnp.maximum, ms)
            w = [jnp.exp2(x - top) for x in ms]
            num = functools.reduce(lambda a, b: a + b, [wp * a for wp, a in zip(w, accs)])
            den = functools.reduce(lambda a, b: a + b, [wp * l for wp, l in zip(w, ls)])
            scatter(o_ref, num / den, dilation, r0, cur)

    for p_idx, (_, dilation) in enumerate(DIL_PATTERNS):
        n_blocks = (n_sub * MAX_DIL // dilation) // DIL_BLOCK
        if n_blocks <= 2:
            per_iter = group // n_blocks

            def body(g, carry, p_idx=p_idx, dilation=dilation, n_blocks=n_blocks, per_iter=per_iter):
                for i in range(per_iter):
                    for blk in range(n_blocks):
                        block(p_idx, dilation, g * per_iter + i, blk)
                return carry

            lax.fori_loop(0, dilation // per_iter, body, 0)
        else:
            per_iter = max(group // dilation, 1)

            def body(g, carry, p_idx=p_idx, dilation=dilation, per_iter=per_iter):
                for i in range(per_iter):
                    for r0 in range(dilation):
                        block(p_idx, dilation, r0, g * per_iter + i)
                return carry

            lax.fori_loop(0, n_blocks // per_iter, body, 0)


def _dil_attention(qkv, tabs, B, S, *, group=32):
    H = N_HEADS_DIL
    n_sub = S // MAX_DIL
    P = len(DIL_PATTERNS)
    kern = functools.partial(_dil_kernel, scale=HEAD_DIM ** -0.5, group=group)
    blk = (None, MAX_DIL, n_sub, HEAD_DIM)
    return pl.pallas_call(
        kern,
        grid=(B, H),
        in_specs=[
            pl.BlockSpec(blk, lambda b, h: (b, 0, 0, h)),
            pl.BlockSpec(blk, lambda b, h: (b, 0, 0, H + h)),
            pl.BlockSpec(blk, lambda b, h: (b, 0, 0, 2 * H + h)),
            pl.BlockSpec((P, None, 2, DIL_BLOCK, 2 * DIL_BLOCK), lambda b, h: (0, h, 0, 0, 0)),
        ],
        out_specs=pl.BlockSpec(blk, lambda b, h: (b, 0, 0, h)),
        out_shape=jax.ShapeDtypeStruct((B, MAX_DIL, n_sub, D_DIL), F32),
        scratch_shapes=[pltpu.VMEM((P - 1, MAX_DIL, n_sub, HEAD_DIM), F32)] * 3,
        compiler_params=_cparams(("parallel", "parallel")),
        name="dil_attention",
    )(qkv, qkv, qkv, tabs)


def _out_proj_kernel(ya_ref, yb_ref, ga_ref, gb_ref, perm_ref, w_ref, x_ref, o_ref):
    tm = ya_ref.shape[0]
    per = REGROUP // MAX_DIL
    for c in range(tm // REGROUP):
        rows = slice(c * REGROUP, (c + 1) * REGROUP)
        ya = ya_ref[rows, :]
        mix_a = (ya * _rms_scale(ya) * ga_ref[...]).astype(BF16)
        yb = jnp.concatenate([yb_ref[r, c * per:(c + 1) * per, :] for r in range(MAX_DIL)], axis=0)
        mix_b = (yb * _rms_scale(yb) * gb_ref[...]).astype(BF16)
        mix_b = jnp.dot(perm_ref[...], mix_b, preferred_element_type=F32).astype(BF16)
        mix = jnp.concatenate([mix_a, mix_b], axis=1)
        o_ref[rows, :] = x_ref[rows, :] + jnp.dot(mix, w_ref[...], preferred_element_type=F32)


def _out_proj(ya, yb, ga, gb, perm, w, x, layer, B, S, *, tm=512):
    T, D = x.shape
    per_b = S // tm
    return pl.pallas_call(
        _out_proj_kernel,
        grid=(T // tm,),
        in_specs=[
            pl.BlockSpec((tm, D_FOX), lambda i: (i, 0)),
            pl.BlockSpec((None, MAX_DIL, tm // MAX_DIL, D_DIL), lambda i: (i // per_b, 0, i % per_b, 0)),
            pl.BlockSpec((None, 1, D_FOX), lambda i: (layer, 0, 0)),
            pl.BlockSpec((None, 1, D_DIL), lambda i: (layer, 0, 0)),
            pl.BlockSpec((REGROUP, REGROUP), lambda i: (0, 0)),
            pl.BlockSpec((None, D_FOX + D_DIL, D), lambda i: (layer, 0, 0)),
            pl.BlockSpec((tm, D), lambda i: (i, 0)),
        ],
        out_specs=pl.BlockSpec((tm, D), lambda i: (i, 0)),
        out_shape=jax.ShapeDtypeStruct((T, D), F32),
        compiler_params=_cparams(("parallel",)),
        name="out_proj",
    )(ya, yb, ga, gb, perm, w, x)


def _mlp_kernel(x_ref, g_ref, w1_ref, w2_ref, fg_ref, o_ref, xn_ref, acc_ref, *, final_norm, sub):
    f = pl.program_id(1)
    tf = w1_ref.shape[1]

    @pl.when(f == 0)
    def _():
        x = x_ref[...]
        xn_ref[...] = (x * _rms_scale(x) * g_ref[...]).astype(BF16)
        acc_ref[...] = x

    upd = None
    for c in range(tf // sub):
        h = jnp.dot(xn_ref[...], w1_ref[:, c * sub:(c + 1) * sub], preferred_element_type=F32)
        h = jnp.square(jnp.maximum(h, 0.0)).astype(BF16)
        d = jnp.dot(h, w2_ref[c * sub:(c + 1) * sub, :], preferred_element_type=F32)
        upd = d if upd is None else upd + d
    acc_ref[...] += upd

    @pl.when(f == pl.num_programs(1) - 1)
    def _():
        y = acc_ref[...]
        if final_norm:
            y = y * _rms_scale(y) * fg_ref[...]
        o_ref[...] = y


def _mlp(x, g, w1, w2, fg, layer, *, final_norm, tm=512, tf=MLP_TF, sub=512):
    T, D = x.shape
    F = w1.shape[2]
    kern = functools.partial(_mlp_kernel, final_norm=final_norm, sub=sub)
    return pl.pallas_call(
        kern,
        grid=(T // tm, F // tf),
        in_specs=[
            pl.BlockSpec((tm, D), lambda i, f: (i, 0)),
            pl.BlockSpec((None, 1, D), lambda i, f: (layer, 0, 0)),
            pl.BlockSpec((None, D, tf), lambda i, f: (layer, 0, f)),
            pl.BlockSpec((None, tf, D), lambda i, f: (layer, f, 0)),
            pl.BlockSpec((1, D), lambda i, f: (0, 0)),
        ],
        out_specs=pl.BlockSpec((tm, D), lambda i, f: (i, 0)),
        out_shape=jax.ShapeDtypeStruct((T, D), F32),
        scratch_shapes=[pltpu.VMEM((tm, D), BF16), pltpu.VMEM((tm, D), F32)],
        compiler_params=_cparams(("parallel", "arbitrary")),
        name="mlp",
    )(x, g, w1, w2, fg)


def kernel(x, norm1_g, w_in, forget_b, rel_bias, outnorm_a_g, outnorm_b_g, w_out,
           norm2_g, w_mlp_in, w_mlp_out, final_norm_g):
    B, S, D = x.shape
    T = B * S
    depth = w_in.shape[0]
    f0 = 3 * D_FOX
    assert S % (MAX_DIL * DIL_BLOCK) == 0 and w_in.shape[2] == f0 + N_HEADS_FOX + 3 * D_DIL

    w_fox = w_in[:, :, :f0].astype(BF16)
    w_dil = w_in[:, :, f0 + N_HEADS_FOX:].astype(BF16)
    w_f = jnp.pad(w_in[:, :, f0:f0 + N_HEADS_FOX], ((0, 0), (0, 0), (0, LANES - N_HEADS_FOX))).astype(BF16)
    fb = jnp.pad(forget_b.astype(F32), ((0, 0), (0, LANES - N_HEADS_FOX))).reshape(depth, 1, LANES)
    w_o = w_out.astype(BF16)
    w1 = w_mlp_in.astype(BF16)
    w2 = w_mlp_out.astype(BF16)
    g1 = norm1_g.astype(F32).reshape(depth, 1, D)
    g2 = norm2_g.astype(F32).reshape(depth, 1, D)
    ga = outnorm_a_g.astype(F32).reshape(depth, 1, D_FOX)
    gb = outnorm_b_g.astype(F32).reshape(depth, 1, D_DIL)
    fg = final_norm_g.astype(F32).reshape(1, D)
    perm = _regroup_matrix()
    tabs = _bias_tables(rel_bias.astype(F32))

    xf = x.reshape(T, D)
    for l in range(depth):
        fox, dil, logf = _in_proj(xf, g1, w_fox, w_dil, w_f, fb, perm, l, B, S)
        cs = _fox_cumsum(logf, B, S)
        ya = _fox_attention(fox.reshape(B, S, 3 * D_FOX), cs.reshape(B, S, LANES), B, S)
        yb = _dil_attention(dil, tabs, B, S)
        xf = _out_proj(ya.reshape(T, D_FOX), yb, ga, gb, perm, w_o, xf, l, B, S)
        xf = _mlp(xf, g2, w1, w2, fg, l, final_norm=(l == depth - 1))
    return xf.reshape(B, S, D)
```

```python
import functools
import math

import numpy as np
import jax
import jax.numpy as jnp
from jax import lax
from jax.experimental import pallas as pl
from jax.experimental.pallas import tpu as pltpu

F32 = jnp.float32
BF16 = jnp.bfloat16

HEAD_DIM = 128
N_HEADS_FOX = 8
N_HEADS_DIL = 8
D_FOX = N_HEADS_FOX * HEAD_DIM
D_DIL = N_HEADS_DIL * HEAD_DIM
DIL_PATTERNS = ((128, 1), (512, 4), (2048, 16))
DIL_BLOCK = 128
MAX_DIL = 16
REGROUP = MAX_DIL * MAX_DIL
REL_BUCKETS = 32
REL_MAX_DISTANCE = 2048
NORM_EPS = 1e-6
NEG_INF = -1e30
LOG2E = math.log2(math.e)

LANES = 128
IN_PROJ_TN = 512
MLP_TF = 1024
VMEM_LIMIT = 56 * 1024 * 1024


def _cparams(sem, flags=None):
    return pltpu.CompilerParams(dimension_semantics=sem, vmem_limit_bytes=VMEM_LIMIT, flags=flags)


def _rms_scale(x):
    return lax.rsqrt(jnp.mean(x * x, axis=-1, keepdims=True) + NORM_EPS)


def _regroup_matrix():
    t = np.arange(REGROUP)
    src = (t % MAX_DIL) * MAX_DIL + t // MAX_DIL
    return jnp.asarray(np.eye(REGROUP, dtype=np.float32)[src], dtype=BF16)


def _in_proj_kernel(x_ref, g_ref, wfox_ref, wdil_ref, wf_ref, fb_ref, perm_ref, fox_ref, dil_ref,
                    logf_ref, *, sub):
    tm = x_ref.shape[0]
    per = REGROUP // MAX_DIL
    for ci in range(tm // REGROUP):
        rows = slice(ci * REGROUP, (ci + 1) * REGROUP)
        x = x_ref[rows, :]
        xn = (x * _rms_scale(x) * g_ref[...]).astype(BF16)
        xnp = jnp.dot(perm_ref[...], xn, preferred_element_type=F32).astype(BF16)
        f = jnp.dot(xn, wf_ref[...], preferred_element_type=F32) + fb_ref[...]
        logf_ref[rows, :] = jnp.minimum(f, 0.0) - jnp.log1p(jnp.exp(-jnp.abs(f)))
        for s in range(0, wfox_ref.shape[1], sub):
            fox_ref[rows, s:s + sub] = jnp.dot(xn, wfox_ref[:, s:s + sub],
                                               preferred_element_type=F32).astype(BF16)
        for s in range(0, wdil_ref.shape[1], sub):
            acc = jnp.dot(xnp, wdil_ref[:, s:s + sub], preferred_element_type=F32)
            for r in range(MAX_DIL):
                dil_ref[r, ci * per:(ci + 1) * per, s:s + sub] = acc[r * per:(r + 1) * per, :]


def _in_proj(x, g, w_fox, w_dil, w_f, fb, perm, layer, B, S, *, tm=REGROUP, sub=IN_PROJ_TN):
    T, D = x.shape
    n_fox = 3 * D_FOX
    n_dil = 3 * D_DIL
    per_b = S // tm
    once = pl.Buffered(1)
    kern = functools.partial(_in_proj_kernel, sub=sub)
    return pl.pallas_call(
        kern,
        grid=(T // tm,),
        in_specs=[
            pl.BlockSpec((tm, D), lambda i: (i, 0)),
            pl.BlockSpec((None, 1, D), lambda i: (layer, 0, 0)),
            pl.BlockSpec((None, D, n_fox), lambda i: (layer, 0, 0), pipeline_mode=once),
            pl.BlockSpec((None, D, n_dil), lambda i: (layer, 0, 0), pipeline_mode=once),
            pl.BlockSpec((None, D, LANES), lambda i: (layer, 0, 0)),
            pl.BlockSpec((None, 1, LANES), lambda i: (layer, 0, 0)),
            pl.BlockSpec((REGROUP, REGROUP), lambda i: (0, 0)),
        ],
        out_specs=[
            pl.BlockSpec((tm, n_fox), lambda i: (i, 0)),
            pl.BlockSpec((None, MAX_DIL, tm // MAX_DIL, n_dil), lambda i: (i // per_b, 0, i % per_b, 0)),
            pl.BlockSpec((tm, LANES), lambda i: (i, 0)),
        ],
        out_shape=[
            jax.ShapeDtypeStruct((T, n_fox), BF16),
            jax.ShapeDtypeStruct((B, MAX_DIL, S // MAX_DIL, n_dil), F32),
            jax.ShapeDtypeStruct((T, LANES), F32),
        ],
        compiler_params=_cparams(("parallel",)),
        name="in_proj",
    )(x, g, w_fox, w_dil, w_f, fb, perm)


def _split3(x):
    hi = x.astype(BF16)
    r1 = x - hi.astype(F32)
    mid = r1.astype(BF16)
    lo = (r1 - mid.astype(F32)).astype(BF16)
    return hi, mid, lo


def _cumsum_kernel(logf_ref, cs_ref, *, chunk):
    S = logf_ref.shape[0]
    n_chunks = S // chunk
    row = lax.broadcasted_iota(jnp.int32, (chunk, chunk), 0)
    col = lax.broadcasted_iota(jnp.int32, (chunk, chunk), 1)
    tri = (col <= row).astype(BF16)
    carry = jnp.zeros((1, LANES), F32)
    for ci in range(n_chunks):
        seg = logf_ref[ci * chunk:(ci + 1) * chunk, :]
        hi, mid, lo = _split3(seg)
        within = (jnp.dot(tri, hi, preferred_element_type=F32)
                  + jnp.dot(tri, mid, preferred_element_type=F32)
                  + jnp.dot(tri, lo, preferred_element_type=F32))
        within = within + carry
        cs_ref[ci * chunk:(ci + 1) * chunk, :] = within
        carry = within[chunk - 1:chunk, :]


def _fox_cumsum(logf, B, S):
    kern = functools.partial(_cumsum_kernel, chunk=128)
    return pl.pallas_call(
        kern,
        grid=(B,),
        in_specs=[pl.BlockSpec((S, LANES), lambda b: (b, 0))],
        out_specs=pl.BlockSpec((S, LANES), lambda b: (b, 0)),
        out_shape=jax.ShapeDtypeStruct((B * S, LANES), F32),
        compiler_params=_cparams(("parallel",)),
        name="fox_cumsum",
    )(logf)


def _fox_head_kernel(q_ref, k_ref, v_ref, cs_ref, o_ref, kaug_ref, vaug_ref, qaug_ref, *, tq, scale):
    h = pl.program_id(1)
    S = k_ref.shape[0]
    tk = tq // 2
    scale2 = scale * LOG2E
    nt = (((1,), (1,)), ((), ()))

    lane = lax.broadcasted_iota(jnp.int32, (tk, LANES), 1)
    for ci in range(S // tk):
        rows = slice(ci * tk, (ci + 1) * tk)
        col = jnp.sum(jnp.where(lane == h, cs_ref[rows, :], 0.0), axis=-1, keepdims=True)
        hi, mid, lo = _split3(col * (-1.0 / scale))
        aug = jnp.where(lane == 0, hi.astype(F32),
                        jnp.where(lane == 1, mid.astype(F32),
                                  jnp.where(lane == 2, lo.astype(F32), 0.0)))
        kaug_ref[rows, :HEAD_DIM] = k_ref[rows, :]
        kaug_ref[rows, HEAD_DIM:] = aug.astype(BF16)
        vaug_ref[rows, :HEAD_DIM] = v_ref[rows, :]
        vaug_ref[rows, HEAD_DIM:] = jnp.ones((tk, LANES), BF16)
        qaug_ref[rows, :HEAD_DIM] = q_ref[rows, :]
        qaug_ref[rows, HEAD_DIM:] = jnp.where(lane < 3, 1.0, 0.0).astype(BF16)

    def scores(q0, blk, row_lo):
        return lax.dot_general(qaug_ref[q0 + row_lo:q0 + tq, :], kaug_ref[blk * tk:(blk + 1) * tk, :],
                               nt, preferred_element_type=F32)

    def softmax(st, m_prev, masked):
        n = st.shape[0]
        tiles = [st[:, j * LANES:(j + 1) * LANES] for j in range(tk // LANES)]
        if masked:
            row = lax.broadcasted_iota(jnp.int32, (n, LANES), 0)
            col = lax.broadcasted_iota(jnp.int32, (n, LANES), 1)
            tiles = [jnp.where(col + j * LANES <= row, t, NEG_INF) for j, t in enumerate(tiles)]
        top = jnp.max(functools.reduce(jnp.maximum, tiles), axis=-1, keepdims=True) * scale2
        if m_prev is None:
            m_new = jnp.maximum(jnp.full((n, LANES), NEG_INF, F32), top)
            alpha = None
        else:
            m_new = jnp.maximum(m_prev, top)
            alpha = jnp.exp2(m_prev - m_new)
        p = jnp.concatenate([jnp.exp2(t * scale2 - m_new).astype(BF16) for t in tiles], axis=1)
        return p, alpha, m_new

    def fold(accl, alpha, p, blk):
        oa = jnp.dot(p, vaug_ref[blk * tk:(blk + 1) * tk, :], preferred_element_type=F32)
        if accl is None:
            return oa
        return jnp.concatenate([alpha, alpha], axis=1) * accl + oa

    for qi in range(S // tq):
        q0 = qi * tq
        n_full = 2 * qi
        m = accl = None
        st_next = scores(q0, 0, 0)
        for blk in range(n_full + 1):
            st = st_next
            st_next = scores(q0, blk + 1, tk if blk == n_full else 0)
            if blk < n_full:
                p, alpha, m = softmax(st, m, False)
            else:
                m_lo = None if m is None else m[:tk]
                m_hi = None if m is None else m[tk:]
                p_lo, a_lo, m_lo = softmax(st[:tk], m_lo, True)
                p_hi, a_hi, m_hi = softmax(st[tk:], m_hi, False)
                p = jnp.concatenate([p_lo, p_hi], axis=0)
                alpha = None if a_lo is None else jnp.concatenate([a_lo, a_hi], axis=0)
                m = jnp.concatenate([m_lo, m_hi], axis=0)
            accl = fold(accl, alpha, p, blk)
        p, alpha, _ = softmax(st_next, m[tk:], True)
        accl_hi = fold(accl[tk:], alpha, p, n_full + 1)
        o_ref[q0:q0 + tk, :] = accl[:tk, :HEAD_DIM] / accl[:tk, HEAD_DIM:]
        o_ref[q0 + tk:q0 + tq, :] = accl_hi[:, :HEAD_DIM] / accl_hi[:, HEAD_DIM:]


def _fox_attention(fox, cs, B, S, *, tq=1024):
    H = N_HEADS_FOX
    kern = functools.partial(_fox_head_kernel, tq=tq, scale=HEAD_DIM ** -0.5)
    return pl.pallas_call(
        kern,
        grid=(B, H),
        in_specs=[
            pl.BlockSpec((None, S, HEAD_DIM), lambda b, h: (b, 0, h)),
            pl.BlockSpec((None, S, HEAD_DIM), lambda b, h: (b, 0, H + h)),
            pl.BlockSpec((None, S, HEAD_DIM), lambda b, h: (b, 0, 2 * H + h)),
            pl.BlockSpec((None, S, LANES), lambda b, h: (b, 0, 0)),
        ],
        out_specs=pl.BlockSpec((None, S, HEAD_DIM), lambda b, h: (b, 0, h)),
        out_shape=jax.ShapeDtypeStruct((B, S, D_FOX), F32),
        scratch_shapes=[
            pltpu.VMEM((S, 2 * HEAD_DIM), BF16),
            pltpu.VMEM((S, 2 * HEAD_DIM), BF16),
            pltpu.VMEM((S, 2 * HEAD_DIM), BF16),
        ],
        compiler_params=_cparams(("parallel", "parallel")),
        name="fox_attention",
    )(fox, fox, fox, cs)


def _rel_bucket_np(dist):
    max_exact = REL_BUCKETS // 2
    d = np.maximum(dist.astype(np.float32), np.float32(1.0))
    ratio = np.log(d / np.float32(max_exact)).astype(np.float32) / np.float32(
        math.log(REL_MAX_DISTANCE / max_exact))
    large = max_exact + (ratio * np.float32(REL_BUCKETS - max_exact)).astype(np.int32)
    large = np.minimum(large, REL_BUCKETS - 1)
    return np.where(dist < max_exact, dist, large)


def _block_perm(dilation):
    n_chunks = MAX_DIL // dilation
    rows = DIL_BLOCK // n_chunks
    a = np.arange(n_chunks)[:, None]
    nn = np.arange(rows)[None, :]
    return (nn * n_chunks + a).reshape(-1)


def _bucket_tables():
    tabs = []
    for window, dilation in DIL_PATTERNS:
        span = window // dilation
        perm = _block_perm(dilation)
        i = perm[:, None]
        j = np.concatenate([perm, DIL_BLOCK + perm])[None, :]
        rel = DIL_BLOCK + i - j
        in_band = (rel >= 0) & (rel <= span)
        bucket = _rel_bucket_np(np.clip(rel, 0, span) * dilation)
        full = np.where(in_band, bucket, -1).astype(np.int32)
        first = np.where(j >= DIL_BLOCK, full, -1).astype(np.int32)
        tabs.append(np.stack([full, first]))
    return np.stack(tabs)


def _bias_table_kernel(rel_bias_ref, idx_ref, tab_ref):
    idx = idx_ref[...]
    for h in range(tab_ref.shape[0]):
        tab = jnp.full(idx.shape, NEG_INF, F32)
        for b in range(REL_BUCKETS):
            tab = jnp.where(idx == b, rel_bias_ref[b, h] * LOG2E, tab)
        tab_ref[h] = tab


def _bias_tables(rel_bias):
    idx = jnp.asarray(_bucket_tables())
    P = len(DIL_PATTERNS)
    blk = (2, DIL_BLOCK, 2 * DIL_BLOCK)
    return pl.pallas_call(
        _bias_table_kernel,
        grid=(P,),
        in_specs=[
            pl.BlockSpec(memory_space=pltpu.SMEM),
            pl.BlockSpec((None,) + blk, lambda p: (p, 0, 0, 0)),
        ],
        out_specs=pl.BlockSpec((None, N_HEADS_DIL) + blk, lambda p: (p, 0, 0, 0, 0)),
        out_shape=jax.ShapeDtypeStruct((P, N_HEADS_DIL) + blk, F32),
        compiler_params=_cparams(("parallel",)),
        name="dil_bias_tables",
    )(rel_bias, idx)


def _dil_kernel(q_ref, k_ref, v_ref, tab_ref, o_ref, acc_ref, l_ref, m_ref, *, scale, group):
    n_sub = q_ref.shape[1]
    n_pat = len(DIL_PATTERNS)
    scale2 = scale * LOG2E
    ones = jnp.ones((2 * DIL_BLOCK, HEAD_DIM), BF16)

    def rows_at(start, rows):
        return pl.ds(start if isinstance(start, int) else pl.multiple_of(start, rows), rows)

    def gather(ref, dilation, r0, starts):
        n_chunks = MAX_DIL // dilation
        rows = DIL_BLOCK // n_chunks
        parts = [ref[r0 + a * dilation, rows_at(s, rows), :]
                 for s in starts for a in range(n_chunks)]
        return parts[0] if len(parts) == 1 else jnp.concatenate(parts, axis=0)

    def scatter(ref, val, dilation, r0, start):
        n_chunks = MAX_DIL // dilation
        rows = DIL_BLOCK // n_chunks
        for a in range(n_chunks):
            ref[r0 + a * dilation, rows_at(start, rows), :] = val[a * rows:(a + 1) * rows]

    def block(p_idx, dilation, r0, blk):
        rows = DIL_BLOCK * dilation // MAX_DIL
        cur = blk * rows
        q = gather(q_ref, dilation, r0, [cur]).astype(BF16)
        if isinstance(blk, int) and blk == 0:
            starts = [cur]
            tab = tab_ref[p_idx, 1, :, DIL_BLOCK:]
        elif isinstance(blk, int):
            starts = [cur - rows, cur]
            tab = tab_ref[p_idx, 0]
        else:
            starts = [jnp.maximum(blk - 1, 0) * rows, cur]
            tab = tab_ref[p_idx, (blk == 0).astype(jnp.int32)]
        k = gather(k_ref, dilation, r0, starts).astype(BF16)
        v = gather(v_ref, dilation, r0, starts).astype(BF16)
        s2 = lax.dot_general(q, k, (((1,), (1,)), ((), ())), preferred_element_type=F32)
        s2 = s2 * scale2 + tab
        m = jnp.max(s2, axis=-1, keepdims=True)
        p = jnp.exp2(s2 - m).astype(BF16)
        ol = jnp.dot(p, jnp.concatenate([v, ones[:v.shape[0]]], axis=1), preferred_element_type=F32)
        if p_idx < n_pat - 1:
            scatter(acc_ref.at[p_idx], ol[:, :HEAD_DIM], dilation, r0, cur)
            scatter(l_ref.at[p_idx], ol[:, HEAD_DIM:], dilation, r0, cur)
            scatter(m_ref.at[p_idx], jnp.broadcast_to(m, (DIL_BLOCK, HEAD_DIM)), dilation, r0, cur)
        else:
            ms = [gather(m_ref.at[p], dilation, r0, [cur]) for p in range(n_pat - 1)] + [m]
            accs = [gather(acc_ref.at[p], dilation, r0, [cur]) for p in range(n_pat - 1)] + [ol[:, :HEAD_DIM]]
            ls = [gather(l_ref.at[p], dilation, r0, [cur]) for p in range(n_pat - 1)] + [ol[:, HEAD_DIM:]]
            top = functools.reduce(jnp.maximum, ms)
            w = [jnp.exp2(x - top) for x in ms]
            num = functools.reduce(lambda a, b: a + b, [wp * a for wp, a in zip(w, accs)])
            den = functools.reduce(lambda a, b: a + b, [wp * l for wp, l in zip(w, ls)])
            scatter(o_ref, num / den, dilation, r0, cur)

    for p_idx, (_, dilation) in enumerate(DIL_PATTERNS):
        n_blocks = (n_sub * MAX_DIL // dilation) // DIL_BLOCK
        if n_blocks <= 2:
            per_iter = group // n_blocks

            def body(g, carry, p_idx=p_idx, dilation=dilation, n_blocks=n_blocks, per_iter=per_iter):
                for i in range(per_iter):
                    for blk in range(n_blocks):
                        block(p_idx, dilation, g * per_iter + i, blk)
                return carry

            lax.fori_loop(0, dilation // per_iter, body, 0)
        else:
            per_iter = max(group // dilation, 1)

            def body(g, carry, p_idx=p_idx, dilation=dilation, per_iter=per_iter):
                for i in range(per_iter):
                    for r0 in range(dilation):
                        block(p_idx, dilation, r0, g * per_iter + i)
                return carry

            lax.fori_loop(0, n_blocks // per_iter, body, 0)


def _dil_attention(qkv, tabs, B, S, *, group=32):
    H = N_HEADS_DIL
    n_sub = S // MAX_DIL
    P = len(DIL_PATTERNS)
    kern = functools.partial(_dil_kernel, scale=HEAD_DIM ** -0.5, group=group)
    blk = (None, MAX_DIL, n_sub, HEAD_DIM)
    return pl.pallas_call(
        kern,
        grid=(B, H),
        in_specs=[
            pl.BlockSpec(blk, lambda b, h: (b, 0, 0, h)),
            pl.BlockSpec(blk, lambda b, h: (b, 0, 0, H + h)),
            pl.BlockSpec(blk, lambda b, h: (b, 0, 0, 2 * H + h)),
            pl.BlockSpec((P, None, 2, DIL_BLOCK, 2 * DIL_BLOCK), lambda b, h: (0, h, 0, 0, 0)),
        ],
        out_specs=pl.BlockSpec(blk, lambda b, h: (b, 0, 0, h)),
        out_shape=jax.ShapeDtypeStruct((B, MAX_DIL, n_sub, D_DIL), F32),
        scratch_shapes=[pltpu.VMEM((P - 1, MAX_DIL, n_sub, HEAD_DIM), F32)] * 3,
        compiler_params=_cparams(("parallel", "parallel")),
        name="dil_attention",
    )(qkv, qkv, qkv, tabs)


def _out_proj_kernel(ya_ref, yb_ref, ga_ref, gb_ref, perm_ref, w_ref, x_ref, o_ref):
    tm = ya_ref.shape[0]
    per = REGROUP // MAX_DIL
    for c in range(tm // REGROUP):
        rows = slice(c * REGROUP, (c + 1) * REGROUP)
        ya = ya_ref[rows, :]
        mix_a = (ya * _rms_scale(ya) * ga_ref[...]).astype(BF16)
        yb = jnp.concatenate([yb_ref[r, c * per:(c + 1) * per, :] for r in range(MAX_DIL)], axis=0)
        mix_b = (yb * _rms_scale(yb) * gb_ref[...]).astype(BF16)
        mix_b = jnp.dot(perm_ref[...], mix_b, preferred_element_type=F32).astype(BF16)
        mix = jnp.concatenate([mix_a, mix_b], axis=1)
        o_ref[rows, :] = x_ref[rows, :] + jnp.dot(mix, w_ref[...], preferred_element_type=F32)


def _out_proj(ya, yb, ga, gb, perm, w, x, layer, B, S, *, tm=512):
    T, D = x.shape
    per_b = S // tm
    return pl.pallas_call(
        _out_proj_kernel,
        grid=(T // tm,),
        in_specs=[
            pl.BlockSpec((tm, D_FOX), lambda i: (i, 0)),
            pl.BlockSpec((None, MAX_DIL, tm // MAX_DIL, D_DIL), lambda i: (i // per_b, 0, i % per_b, 0)),
            pl.BlockSpec((None, 1, D_FOX), lambda i: (layer, 0, 0)),
            pl.BlockSpec((None, 1, D_DIL), lambda i: (layer, 0, 0)),
            pl.BlockSpec((REGROUP, REGROUP), lambda i: (0, 0)),
            pl.BlockSpec((None, D_FOX + D_DIL, D), lambda i: (layer, 0, 0)),
            pl.BlockSpec((tm, D), lambda i: (i, 0)),
        ],
        out_specs=pl.BlockSpec((tm, D), lambda i: (i, 0)),
        out_shape=jax.ShapeDtypeStruct((T, D), F32),
        compiler_params=_cparams(("parallel",)),
        name="out_proj",
    )(ya, yb, ga, gb, perm, w, x)


def _mlp_kernel(x_ref, g_ref, w1_ref, w2_ref, fg_ref, o_ref, xn_ref, acc_ref, *, final_norm, sub):
    f = pl.program_id(1)
    tf = w1_ref.shape[1]

    @pl.when(f == 0)
    def _():
        x = x_ref[...]
        xn_ref[...] = (x * _rms_scale(x) * g_ref[...]).astype(BF16)
        acc_ref[...] = x

    upd = None
    for c in range(tf // sub):
        h = jnp.dot(xn_ref[...], w1_ref[:, c * sub:(c + 1) * sub], preferred_element_type=F32)
        h = jnp.square(jnp.maximum(h, 0.0)).astype(BF16)
        d = jnp.dot(h, w2_ref[c * sub:(c + 1) * sub, :], preferred_element_type=F32)
        upd = d if upd is None else upd + d
    acc_ref[...] += upd

    @pl.when(f == pl.num_programs(1) - 1)
    def _():
        y = acc_ref[...]
        if final_norm:
            y = y * _rms_scale(y) * fg_ref[...]
        o_ref[...] = y


def _mlp(x, g, w1, w2, fg, layer, *, final_norm, tm=512, tf=MLP_TF, sub=512):
    T, D = x.shape
    F = w1.shape[2]
    kern = functools.partial(_mlp_kernel, final_norm=final_norm, sub=sub)
    return pl.pallas_call(
        kern,
        grid=(T // tm, F // tf),
        in_specs=[
            pl.BlockSpec((tm, D), lambda i, f: (i, 0)),
            pl.BlockSpec((None, 1, D), lambda i, f: (layer, 0, 0)),
            pl.BlockSpec((None, D, tf), lambda i, f: (layer, 0, f)),
            pl.BlockSpec((None, tf, D), lambda i, f: (layer, f, 0)),
            pl.BlockSpec((1, D), lambda i, f: (0, 0)),
        ],
        out_specs=pl.BlockSpec((tm, D), lambda i, f: (i, 0)),
        out_shape=jax.ShapeDtypeStruct((T, D), F32),
        scratch_shapes=[pltpu.VMEM((tm, D), BF16), pltpu.VMEM((tm, D), F32)],
        compiler_params=_cparams(("parallel", "arbitrary")),
        name="mlp",
    )(x, g, w1, w2, fg)


def kernel(x, norm1_g, w_in, forget_b, rel_bias, outnorm_a_g, outnorm_b_g, w_out,
           norm2_g, w_mlp_in, w_mlp_out, final_norm_g):
    B, S, D = x.shape
    T = B * S
    depth = w_in.shape[0]
    f0 = 3 * D_FOX
    assert S % (MAX_DIL * DIL_BLOCK) == 0 and w_in.shape[2] == f0 + N_HEADS_FOX + 3 * D_DIL

    w_fox = w_in[:, :, :f0].astype(BF16)
    w_dil = w_in[:, :, f0 + N_HEADS_FOX:].astype(BF16)
    w_f = jnp.pad(w_in[:, :, f0:f0 + N_HEADS_FOX], ((0, 0), (0, 0), (0, LANES - N_HEADS_FOX))).astype(BF16)
    fb = jnp.pad(forget_b.astype(F32), ((0, 0), (0, LANES - N_HEADS_FOX))).reshape(depth, 1, LANES)
    w_o = w_out.astype(BF16)
    w1 = w_mlp_in.astype(BF16)
    w2 = w_mlp_out.astype(BF16)
    g1 = norm1_g.astype(F32).reshape(depth, 1, D)
    g2 = norm2_g.astype(F32).reshape(depth, 1, D)
    ga = outnorm_a_g.astype(F32).reshape(depth, 1, D_FOX)
    gb = outnorm_b_g.astype(F32).reshape(depth, 1, D_DIL)
    fg = final_norm_g.astype(F32).reshape(1, D)
    perm = _regroup_matrix()
    tabs = _bias_tables(rel_bias.astype(F32))

    xf = x.reshape(T, D)
    for l in range(depth):
        fox, dil, logf = _in_proj(xf, g1, w_fox, w_dil, w_f, fb, perm, l, B, S)
        cs = _fox_cumsum(logf, B, S)
        ya = _fox_attention(fox.reshape(B, S, 3 * D_FOX), cs.reshape(B, S, LANES), B, S)
        yb = _dil_attention(dil, tabs, B, S)
        xf = _out_proj(ya.reshape(T, D_FOX), yb, ga, gb, perm, w_o, xf, l, B, S)
        xf = _mlp(xf, g2, w1, w2, fg, l, final_norm=(l == depth - 1))
    return xf.reshape(B, S, D)
```

```python
import functools
import math

import numpy as np
import jax
import jax.numpy as jnp
from jax import lax
from jax.experimental import pallas as pl
from jax.experimental.pallas import tpu as pltpu

F32 = jnp.float32
BF16 = jnp.bfloat16

HEAD_DIM = 128
N_HEADS_FOX = 8
N_HEADS_DIL = 8
D_FOX = N_HEADS_FOX * HEAD_DIM
D_DIL = N_HEADS_DIL * HEAD_DIM
DIL_PATTERNS = ((128, 1), (512, 4), (2048, 16))
DIL_BLOCK = 128
MAX_DIL = 16
REGROUP = MAX_DIL * MAX_DIL
REL_BUCKETS = 32
REL_MAX_DISTANCE = 2048
NORM_EPS = 1e-6
NEG_INF = -1e30
LOG2E = math.log2(math.e)

LANES = 128
IN_PROJ_TN = 512
MLP_TF = 1024
VMEM_LIMIT = 56 * 1024 * 1024


def _cparams(sem, flags=None):
    return pltpu.CompilerParams(dimension_semantics=sem, vmem_limit_bytes=VMEM_LIMIT, flags=flags)


def _rms_scale(x):
    return lax.rsqrt(jnp.mean(x * x, axis=-1, keepdims=True) + NORM_EPS)


def _regroup_matrix():
    t = np.arange(REGROUP)
    src = (t % MAX_DIL) * MAX_DIL + t // MAX_DIL
    return jnp.asarray(np.eye(REGROUP, dtype=np.float32)[src], dtype=BF16)


def _in_proj_kernel(x_ref, g_ref, wfox_ref, wdil_ref, wf_ref, fb_ref, perm_ref, fox_ref, dil_ref,
                    logf_ref, *, sub):
    tm = x_ref.shape[0]
    per = REGROUP // MAX_DIL
    for ci in range(tm // REGROUP):
        rows = slice(ci * REGROUP, (ci + 1) * REGROUP)
        x = x_ref[rows, :]
        xn = (x * _rms_scale(x) * g_ref[...]).astype(BF16)
        xnp = jnp.dot(perm_ref[...], xn, preferred_element_type=F32).astype(BF16)
        f = jnp.dot(xn, wf_ref[...], preferred_element_type=F32) + fb_ref[...]
        logf_ref[rows, :] = jnp.minimum(f, 0.0) - jnp.log1p(jnp.exp(-jnp.abs(f)))
        for s in range(0, wfox_ref.shape[1], sub):
            fox_ref[rows, s:s + sub] = jnp.dot(xn, wfox_ref[:, s:s + sub],
                                               preferred_element_type=F32).astype(BF16)
        for s in range(0, wdil_ref.shape[1], sub):
            acc = jnp.dot(xnp, wdil_ref[:, s:s + sub], preferred_element_type=F32)
            for r in range(MAX_DIL):
                dil_ref[r, ci * per:(ci + 1) * per, s:s + sub] = acc[r * per:(r + 1) * per, :]


def _in_proj(x, g, w_fox, w_dil, w_f, fb, perm, layer, B, S, *, tm=REGROUP, sub=IN_PROJ_TN):
    T, D = x.shape
    n_fox = 3 * D_FOX
    n_dil = 3 * D_DIL
    per_b = S // tm
    once = pl.Buffered(1)
    kern = functools.partial(_in_proj_kernel, sub=sub)
    return pl.pallas_call(
        kern,
        grid=(T // tm,),
        in_specs=[
            pl.BlockSpec((tm, D), lambda i: (i, 0)),
            pl.BlockSpec((None, 1, D), lambda i: (layer, 0, 0)),
            pl.BlockSpec((None, D, n_fox), lambda i: (layer, 0, 0), pipeline_mode=once),
            pl.BlockSpec((None, D, n_dil), lambda i: (layer, 0, 0), pipeline_mode=once),
            pl.BlockSpec((None, D, LANES), lambda i: (layer, 0, 0)),
            pl.BlockSpec((None, 1, LANES), lambda i: (layer, 0, 0)),
            pl.BlockSpec((REGROUP, REGROUP), lambda i: (0, 0)),
        ],
        out_specs=[
            pl.BlockSpec((tm, n_fox), lambda i: (i, 0)),
            pl.BlockSpec((None, MAX_DIL, tm // MAX_DIL, n_dil), lambda i: (i // per_b, 0, i % per_b, 0)),
            pl.BlockSpec((tm, LANES), lambda i: (i, 0)),
        ],
        out_shape=[
            jax.ShapeDtypeStruct((T, n_fox), BF16),
            jax.ShapeDtypeStruct((B, MAX_DIL, S // MAX_DIL, n_dil), F32),
            jax.ShapeDtypeStruct((T, LANES), F32),
        ],
        compiler_params=_cparams(("parallel",)),
        name="in_proj",
    )(x, g, w_fox, w_dil, w_f, fb, perm)


def _split3(x):
    hi = x.astype(BF16)
    r1 = x - hi.astype(F32)
    mid = r1.astype(BF16)
    lo = (r1 - mid.astype(F32)).astype(BF16)
    return hi, mid, lo


def _cumsum_kernel(logf_ref, cs_ref, *, chunk):
    S = logf_ref.shape[0]
    n_chunks = S // chunk
    row = lax.broadcasted_iota(jnp.int32, (chunk, chunk), 0)
    col = lax.broadcasted_iota(jnp.int32, (chunk, chunk), 1)
    tri = (col <= row).astype(BF16)
    carry = jnp.zeros((1, LANES), F32)
    for ci in range(n_chunks):
        seg = logf_ref[ci * chunk:(ci + 1) * chunk, :]
        hi, mid, lo = _split3(seg)
        within = (jnp.dot(tri, hi, preferred_element_type=F32)
                  + jnp.dot(tri, mid, preferred_element_type=F32)
                  + jnp.dot(tri, lo, preferred_element_type=F32))
        within = within + carry
        cs_ref[ci * chunk:(ci + 1) * chunk, :] = within
        carry = within[chunk - 1:chunk, :]


def _fox_cumsum(logf, B, S):
    kern = functools.partial(_cumsum_kernel, chunk=128)
    return pl.pallas_call(
        kern,
        grid=(B,),
        in_specs=[pl.BlockSpec((S, LANES), lambda b: (b, 0))],
        out_specs=pl.BlockSpec((S, LANES), lambda b: (b, 0)),
        out_shape=jax.ShapeDtypeStruct((B * S, LANES), F32),
        compiler_params=_cparams(("parallel",)),
        name="fox_cumsum",
    )(logf)


def _fox_head_kernel(q_ref, k_ref, v_ref, cs_ref, o_ref, kaug_ref, vaug_ref, qaug_ref, *, tq, scale):
    h = pl.program_id(1)
    S = k_ref.shape[0]
    tk = tq // 2
    scale2 = scale * LOG2E
    nt = (((1,), (1,)), ((), ()))

    lane = lax.broadcasted_iota(jnp.int32, (tk, LANES), 1)
    for ci in range(S // tk):
        rows = slice(ci * tk, (ci + 1) * tk)
        col = jnp.sum(jnp.where(lane == h, cs_ref[rows, :], 0.0), axis=-1, keepdims=True)
        hi, mid, lo = _split3(col * (-1.0 / scale))
        aug = jnp.where(lane == 0, hi.astype(F32),
                        jnp.where(lane == 1, mid.astype(F32),
                                  jnp.where(lane == 2, lo.astype(F32), 0.0)))
        kaug_ref[rows, :HEAD_DIM] = k_ref[rows, :]
        kaug_ref[rows, HEAD_DIM:] = aug.astype(BF16)
        vaug_ref[rows, :HEAD_DIM] = v_ref[rows, :]
        vaug_ref[rows, HEAD_DIM:] = jnp.ones((tk, LANES), BF16)
        qaug_ref[rows, :HEAD_DIM] = q_ref[rows, :]
        qaug_ref[rows, HEAD_DIM:] = jnp.where(lane < 3, 1.0, 0.0).astype(BF16)

    def scores(q0, blk, row_lo):
        return lax.dot_general(qaug_ref[q0 + row_lo:q0 + tq, :], kaug_ref[blk * tk:(blk + 1) * tk, :],
                               nt, preferred_element_type=F32)

    def softmax(st, m_prev, masked):
        n = st.shape[0]
        tiles = [st[:, j * LANES:(j + 1) * LANES] for j in range(tk // LANES)]
        if masked:
            row = lax.broadcasted_iota(jnp.int32, (n, LANES), 0)
            col = lax.broadcasted_iota(jnp.int32, (n, LANES), 1)
            tiles = [jnp.where(col + j * LANES <= row, t, NEG_INF) for j, t in enumerate(tiles)]
        top = jnp.max(functools.reduce(jnp.maximum, tiles), axis=-1, keepdims=True) * scale2
        if m_prev is None:
            m_new = jnp.maximum(jnp.full((n, LANES), NEG_INF, F32), top)
            alpha = None
        else:
            m_new = jnp.maximum(m_prev, top)
            alpha = jnp.exp2(m_prev - m_new)
        p = jnp.concatenate([jnp.exp2(t * scale2 - m_new).astype(BF16) for t in tiles], axis=1)
        return p, alpha, m_new

    def fold(accl, alpha, p, blk):
        oa = jnp.dot(p, vaug_ref[blk * tk:(blk + 1) * tk, :], preferred_element_type=F32)
        if accl is None:
            return oa
        return jnp.concatenate([alpha, alpha], axis=1) * accl + oa

    for qi in range(S // tq):
        q0 = qi * tq
        n_full = 2 * qi
        m = accl = None
        st_next = scores(q0, 0, 0)
        for blk in range(n_full + 1):
            st = st_next
            st_next = scores(q0, blk + 1, tk if blk == n_full else 0)
            if blk < n_full:
                p, alpha, m = softmax(st, m, False)
            else:
                m_lo = None if m is None else m[:tk]
                m_hi = None if m is None else m[tk:]
                p_lo, a_lo, m_lo = softmax(st[:tk], m_lo, True)
                p_hi, a_hi, m_hi = softmax(st[tk:], m_hi, False)
                p = jnp.concatenate([p_lo, p_hi], axis=0)
                alpha = None if a_lo is None else jnp.concatenate([a_lo, a_hi], axis=0)
                m = jnp.concatenate([m_lo, m_hi], axis=0)
            accl = fold(accl, alpha, p, blk)
        p, alpha, _ = softmax(st_next, m[tk:], True)
        accl_hi = fold(accl[tk:], alpha, p, n_full + 1)
        o_ref[q0:q0 + tk, :] = accl[:tk, :HEAD_DIM] / accl[:tk, HEAD_DIM:]
        o_ref[q0 + tk:q0 + tq, :] = accl_hi[:, :HEAD_DIM] / accl_hi[:, HEAD_DIM:]


def _fox_attention(fox, cs, B, S, *, tq=1024):
    H = N_HEADS_FOX
    kern = functools.partial(_fox_head_kernel, tq=tq, scale=HEAD_DIM ** -0.5)
    return pl.pallas_call(
        kern,
        grid=(B, H),
        in_specs=[
            pl.BlockSpec((None, S, HEAD_DIM), lambda b, h: (b, 0, h)),
            pl.BlockSpec((None, S, HEAD_DIM), lambda b, h: (b, 0, H + h)),
            pl.BlockSpec((None, S, HEAD_DIM), lambda b, h: (b, 0, 2 * H + h)),
            pl.BlockSpec((None, S, LANES), lambda b, h: (b, 0, 0)),
        ],
        out_specs=pl.BlockSpec((None, S, HEAD_DIM), lambda b, h: (b, 0, h)),
        out_shape=jax.ShapeDtypeStruct((B, S, D_FOX), F32),
        scratch_shapes=[
            pltpu.VMEM((S, 2 * HEAD_DIM), BF16),
            pltpu.VMEM((S, 2 * HEAD_DIM), BF16),
            pltpu.VMEM((S, 2 * HEAD_DIM), BF16),
        ],
        compiler_params=_cparams(("parallel", "parallel")),
        name="fox_attention",
    )(fox, fox, fox, cs)


def _rel_bucket_np(dist):
    max_exact = REL_BUCKETS // 2
    d = np.maximum(dist.astype(np.float32), np.float32(1.0))
    ratio = np.log(d / np.float32(max_exact)).astype(np.float32) / np.float32(
        math.log(REL_MAX_DISTANCE / max_exact))
    large = max_exact + (ratio * np.float32(REL_BUCKETS - max_exact)).astype(np.int32)
    large = np.minimum(large, REL_BUCKETS - 1)
    return np.where(dist < max_exact, dist, large)


def _block_perm(dilation):
    n_chunks = MAX_DIL // dilation
    rows = DIL_BLOCK // n_chunks
    a = np.arange(n_chunks)[:, None]
    nn = np.arange(rows)[None, :]
    return (nn * n_chunks + a).reshape(-1)


def _bucket_tables():
    tabs = []
    for window, dilation in DIL_PATTERNS:
        span = window // dilation
        perm = _block_perm(dilation)
        i = perm[:, None]
        j = np.concatenate([perm, DIL_BLOCK + perm])[None, :]
        rel = DIL_BLOCK + i - j
        in_band = (rel >= 0) & (rel <= span)
        bucket = _rel_bucket_np(np.clip(rel, 0, span) * dilation)
        full = np.where(in_band, bucket, -1).astype(np.int32)
        first = np.where(j >= DIL_BLOCK, full, -1).astype(np.int32)
        tabs.append(np.stack([full, first]))
    return np.stack(tabs)


def _bias_table_kernel(rel_bias_ref, idx_ref, tab_ref):
    idx = idx_ref[...]
    for h in range(tab_ref.shape[0]):
        tab = jnp.full(idx.shape, NEG_INF, F32)
        for b in range(REL_BUCKETS):
            tab = jnp.where(idx == b, rel_bias_ref[b, h] * LOG2E, tab)
        tab_ref[h] = tab


def _bias_tables(rel_bias):
    idx = jnp.asarray(_bucket_tables())
    P = len(DIL_PATTERNS)
    blk = (2, DIL_BLOCK, 2 * DIL_BLOCK)
    return pl.pallas_call(
        _bias_table_kernel,
        grid=(P,),
        in_specs=[
            pl.BlockSpec(memory_space=pltpu.SMEM),
            pl.BlockSpec((None,) + blk, lambda p: (p, 0, 0, 0)),
        ],
        out_specs=pl.BlockSpec((None, N_HEADS_DIL) + blk, lambda p: (p, 0, 0, 0, 0)),
        out_shape=jax.ShapeDtypeStruct((P, N_HEADS_DIL) + blk, F32),
        compiler_params=_cparams(("parallel",)),
        name="dil_bias_tables",
    )(rel_bias, idx)


def _dil_kernel(q_ref, k_ref, v_ref, tab_ref, o_ref, acc_ref, l_ref, m_ref, *, scale, group):
    n_sub = q_ref.shape[1]
    n_pat = len(DIL_PATTERNS)
    scale2 = scale * LOG2E
    ones = jnp.ones((2 * DIL_BLOCK, HEAD_DIM), BF16)

    def rows_at(start, rows):
        return pl.ds(start if isinstance(start, int) else pl.multiple_of(start, rows), rows)

    def gather(ref, dilation, r0, starts):
        n_chunks = MAX_DIL // dilation
        rows = DIL_BLOCK // n_chunks
        parts = [ref[r0 + a * dilation, rows_at(s, rows), :]
                 for s in starts for a in range(n_chunks)]
        return parts[0] if len(parts) == 1 else jnp.concatenate(parts, axis=0)

    def scatter(ref, val, dilation, r0, start):
        n_chunks = MAX_DIL // dilation
        rows = DIL_BLOCK // n_chunks
        for a in range(n_chunks):
            ref[r0 + a * dilation, rows_at(start, rows), :] = val[a * rows:(a + 1) * rows]

    def block(p_idx, dilation, r0, blk):
        rows = DIL_BLOCK * dilation // MAX_DIL
        cur = blk * rows
        q = gather(q_ref, dilation, r0, [cur]).astype(BF16)
        if isinstance(blk, int) and blk == 0:
            starts = [cur]
            tab = tab_ref[p_idx, 1, :, DIL_BLOCK:]
        elif isinstance(blk, int):
            starts = [cur - rows, cur]
            tab = tab_ref[p_idx, 0]
        else:
            starts = [jnp.maximum(blk - 1, 0) * rows, cur]
            tab = tab_ref[p_idx, (blk == 0).astype(jnp.int32)]
        k = gather(k_ref, dilation, r0, starts).astype(BF16)
        v = gather(v_ref, dilation, r0, starts).astype(BF16)
        s2 = lax.dot_general(q, k, (((1,), (1,)), ((), ())), preferred_element_type=F32)
        s2 = s2 * scale2 + tab
        m = jnp.max(s2, axis=-1, keepdims=True)
        p = jnp.exp2(s2 - m).astype(BF16)
        ol = jnp.dot(p, jnp.concatenate([v, ones[:v.shape[0]]], axis=1), preferred_element_type=F32)
        if p_idx < n_pat - 1:
            scatter(acc_ref.at[p_idx], ol[:, :HEAD_DIM], dilation, r0, cur)
            scatter(l_ref.at[p_idx], ol[:, HEAD_DIM:], dilation, r0, cur)
            scatter(m_ref.at[p_idx], jnp.broadcast_to(m, (DIL_BLOCK, HEAD_DIM)), dilation, r0, cur)
        else:
            ms = [gather(m_ref.at[p], dilation, r0, [cur]) for p in range(n_pat - 1)] + [m]
            accs = [gather(acc_ref.at[p], dilation, r0, [cur]) for p in range(n_pat - 1)] + [ol[:, :HEAD_DIM]]
            ls = [gather(l_ref.at[p], dilation, r0, [cur]) for p in range(n_pat - 1)] + [ol[:, HEAD_DIM:]]
            top = functools.reduce(jnp.maximum, ms)
            w = [jnp.exp2(x - top) for x in ms]
            num = functools.reduce(lambda a, b: a + b, [wp * a for wp, a in zip(w, accs)])
            den = functools.reduce(lambda a, b: a + b, [wp * l for wp, l in zip(w, ls)])
            scatter(o_ref, num / den, dilation, r0, cur)

    for p_idx, (_, dilation) in enumerate(DIL_PATTERNS):
        n_blocks = (n_sub * MAX_DIL // dilation) // DIL_BLOCK
        if n_blocks <= 2:
            per_iter = group // n_blocks

            def body(g, carry, p_idx=p_idx, dilation=dilation, n_blocks=n_blocks, per_iter=per_iter):
                for i in range(per_iter):
                    for blk in range(n_blocks):
                        block(p_idx, dilation, g * per_iter + i, blk)
                return carry

            lax.fori_loop(0, dilation // per_iter, body, 0)
        else:
            per_iter = max(group // dilation, 1)

            def body(g, carry, p_idx=p_idx, dilation=dilation, per_iter=per_iter):
                for i in range(per_iter):
                    for r0 in range(dilation):
                        block(p_idx, dilation, r0, g * per_iter + i)
                return carry

            lax.fori_loop(0, n_blocks // per_iter, body, 0)


def _dil_attention(qkv, tabs, B, S, *, group=32):
    H = N_HEADS_DIL
    n_sub = S // MAX_DIL
    P = len(DIL_PATTERNS)
    kern = functools.partial(_dil_kernel, scale=HEAD_DIM ** -0.5, group=group)
    blk = (None, MAX_DIL, n_sub, HEAD_DIM)
    return pl.pallas_call(
        kern,
        grid=(B, H),
        in_specs=[
            pl.BlockSpec(blk, lambda b, h: (b, 0, 0, h)),
            pl.BlockSpec(blk, lambda b, h: (b, 0, 0, H + h)),
            pl.BlockSpec(blk, lambda b, h: (b, 0, 0, 2 * H + h)),
            pl.BlockSpec((P, None, 2, DIL_BLOCK, 2 * DIL_BLOCK), lambda b, h: (0, h, 0, 0, 0)),
        ],
        out_specs=pl.BlockSpec(blk, lambda b, h: (b, 0, 0, h)),
        out_shape=jax.ShapeDtypeStruct((B, MAX_DIL, n_sub, D_DIL), F32),
        scratch_shapes=[pltpu.VMEM((P - 1, MAX_DIL, n_sub, HEAD_DIM), F32)] * 3,
        compiler_params=_cparams(("parallel", "parallel")),
        name="dil_attention",
    )(qkv, qkv, qkv, tabs)


def _out_proj_kernel(ya_ref, yb_ref, ga_ref, gb_ref, perm_ref, w_ref, x_ref, o_ref):
    tm = ya_ref.shape[0]
    per = REGROUP // MAX_DIL
    for c in range(tm // REGROUP):
        rows = slice(c * REGROUP, (c + 1) * REGROUP)
        ya = ya_ref[rows, :]
        mix_a = (ya * _rms_scale(ya) * ga_ref[...]).astype(BF16)
        yb = jnp.concatenate([yb_ref[r, c * per:(c + 1) * per, :] for r in range(MAX_DIL)], axis=0)
        mix_b = (yb * _rms_scale(yb) * gb_ref[...]).astype(BF16)
        mix_b = jnp.dot(perm_ref[...], mix_b, preferred_element_type=F32).astype(BF16)
        mix = jnp.concatenate([mix_a, mix_b], axis=1)
        o_ref[rows, :] = x_ref[rows, :] + jnp.dot(mix, w_ref[...], preferred_element_type=F32)


def _out_proj(ya, yb, ga, gb, perm, w, x, layer, B, S, *, tm=512):
    T, D = x.shape
    per_b = S // tm
    return pl.pallas_call(
        _out_proj_kernel,
        grid=(T // tm,),
        in_specs=[
            pl.BlockSpec((tm, D_FOX), lambda i: (i, 0)),
            pl.BlockSpec((None, MAX_DIL, tm // MAX_DIL, D_DIL), lambda i: (i // per_b, 0, i % per_b, 0)),
            pl.BlockSpec((None, 1, D_FOX), lambda i: (layer, 0, 0)),
            pl.BlockSpec((None, 1, D_DIL), lambda i: (layer, 0, 0)),
            pl.BlockSpec((REGROUP, REGROUP), lambda i: (0, 0)),
            pl.BlockSpec((None, D_FOX + D_DIL, D), lambda i: (layer, 0, 0)),
            pl.BlockSpec((tm, D), lambda i: (i, 0)),
        ],
        out_specs=pl.BlockSpec((tm, D), lambda i: (i, 0)),
        out_shape=jax.ShapeDtypeStruct((T, D), F32),
        compiler_params=_cparams(("parallel",)),
        name="out_proj",
    )(ya, yb, ga, gb, perm, w, x)


def _mlp_kernel(x_ref, g_ref, w1_ref, w2_ref, fg_ref, o_ref, xn_ref, acc_ref, *, final_norm, sub):
    f = pl.program_id(1)
    tf = w1_ref.shape[1]

    @pl.when(f == 0)
    def _():
        x = x_ref[...]
        xn_ref[...] = (x * _rms_scale(x) * g_ref[...]).astype(BF16)
        acc_ref[...] = x

    upd = None
    for c in range(tf // sub):
        h = jnp.dot(xn_ref[...], w1_ref[:, c * sub:(c + 1) * sub], preferred_element_type=F32)
        h = jnp.square(jnp.maximum(h, 0.0)).astype(BF16)
        d = jnp.dot(h, w2_ref[c * sub:(c + 1) * sub, :], preferred_element_type=F32)
        upd = d if upd is None else upd + d
    acc_ref[...] += upd

    @pl.when(f == pl.num_programs(1) - 1)
    def _():
        y = acc_ref[...]
        if final_norm:
            y = y * _rms_scale(y) * fg_ref[...]
        o_ref[...] = y


def _mlp(x, g, w1, w2, fg, layer, *, final_norm, tm=512, tf=MLP_TF, sub=512):
    T, D = x.shape
    F = w1.shape[2]
    kern = functools.partial(_mlp_kernel, final_norm=final_norm, sub=sub)
    return pl.pallas_call(
        kern,
        grid=(T // tm, F // tf),
        in_specs=[
            pl.BlockSpec((tm, D), lambda i, f: (i, 0)),
            pl.BlockSpec((None, 1, D), lambda i, f: (layer, 0, 0)),
            pl.BlockSpec((None, D, tf), lambda i, f: (layer, 0, f)),
            pl.BlockSpec((None, tf, D), lambda i, f: (layer, f, 0)),
            pl.BlockSpec((1, D), lambda i, f: (0, 0)),
        ],
        out_specs=pl.BlockSpec((tm, D), lambda i, f: (i, 0)),
        out_shape=jax.ShapeDtypeStruct((T, D), F32),
        scratch_shapes=[pltpu.VMEM((tm, D), BF16), pltpu.VMEM((tm, D), F32)],
        compiler_params=_cparams(("parallel", "arbitrary")),
        name="mlp",
    )(x, g, w1, w2, fg)


def kernel(x, norm1_g, w_in, forget_b, rel_bias, outnorm_a_g, outnorm_b_g, w_out,
           norm2_g, w_mlp_in, w_mlp_out, final_norm_g):
    B, S, D = x.shape
    T = B * S
    depth = w_in.shape[0]
    f0 = 3 * D_FOX
    assert S % (MAX_DIL * DIL_BLOCK) == 0 and w_in.shape[2] == f0 + N_HEADS_FOX + 3 * D_DIL

    w_fox = w_in.astype(BF16)
    w_dil = w_fox[:, :, f0 + N_HEADS_FOX:]
    w_f = jnp.pad(w_in[:, :, f0:f0 + N_HEADS_FOX], ((0, 0), (0, 0), (0, LANES - N_HEADS_FOX))).astype(BF16)
    fb = jnp.pad(forget_b.astype(F32), ((0, 0), (0, LANES - N_HEADS_FOX))).reshape(depth, 1, LANES)
    w_o = w_out.astype(BF16)
    w1 = w_mlp_in.astype(BF16)
    w2 = w_mlp_out.astype(BF16)
    g1 = norm1_g.astype(F32).reshape(depth, 1, D)
    g2 = norm2_g.astype(F32).reshape(depth, 1, D)
    ga = outnorm_a_g.astype(F32).reshape(depth, 1, D_FOX)
    gb = outnorm_b_g.astype(F32).reshape(depth, 1, D_DIL)
    fg = final_norm_g.astype(F32).reshape(1, D)
    perm = _regroup_matrix()
    tabs = _bias_tables(rel_bias.astype(F32))

    xf = x.reshape(T, D)
    for l in range(depth):
        fox, dil, logf = _in_proj(xf, g1, w_fox, w_dil, w_f, fb, perm, l, B, S)
        cs = _fox_cumsum(logf, B, S)
        ya = _fox_attention(fox.reshape(B, S, 3 * D_FOX), cs.reshape(B, S, LANES), B, S)
        yb = _dil_attention(dil, tabs, B, S)
        xf = _out_proj(ya.reshape(T, D_FOX), yb, ga, gb, perm, w_o, xf, l, B, S)
        xf = _mlp(xf, g2, w1, w2, fg, l, final_norm=(l == depth - 1))
    return xf.reshape(B, S, D)
```
